```python
import math
import jax
import jax.numpy as jnp
from jax import lax
import numpy as np

D_MODEL = 1024
BATCH = 2
SEQ = 8192
DEPTH = 2
DEC_BATCH = 16
DEC_SEQ = 64
PAST_LEN = 1024

CHUNK = 64
QBLOCK = 128
MLA_HEADS = 8
MLA_NOPE = 64
MLA_ROPE = 32
MLA_V = 64
MLA_Q_LORA = 384
MLA_KV_LORA = 256
MLA_SCALE = (MLA_NOPE + MLA_ROPE) ** -0.5
ROPE_THETA = 10000.0
CM_CH = 512
CM_WIDTH = 31
GDN_HEADS = 4
GDN_DK = 128
GDN_DV = 128
GDN_CONV = 4
GDN_QKV = GDN_HEADS * (2 * GDN_DK + GDN_DV)
MEM_LEN = 256
XA_HEADS = 4
XA_DIM = D_MODEL // XA_HEADS
MOE_GROUPS = 4
MOE_PER_GROUP = 4
MOE_EXPERTS = MOE_GROUPS * MOE_PER_GROUP
MOE_TOPK = 2
MOE_FF = 256
N_BRANCH = 3
DN_ALPHA = (2 * DEPTH) ** 0.25
DN_BETA = (8 * DEPTH) ** -0.25
LN_EPS = 1e-5
RMS_EPS = 1e-6
IN_SIZES = (N_BRANCH * D_MODEL, MLA_Q_LORA, MLA_KV_LORA, MLA_ROPE, 2 * CM_CH,
            GDN_QKV, GDN_HEADS * GDN_DV, GDN_HEADS, GDN_HEADS)
IN_COLS = sum(IN_SIZES)

kernel_name = 'hybrid_stream_mla_conformer_gdn_hmoe_step'


def layer_norm(x, g, b):
    xf = x.astype(jnp.float32)
    xc = xf - xf.mean(-1, keepdims=True)
    var = (xc * xc).mean(-1, keepdims=True)
    return (xc * lax.rsqrt(var + LN_EPS) * g + b).astype(x.dtype)


def rms_norm(x, g):
    xf = x.astype(jnp.float32)
    return (xf * lax.rsqrt((xf * xf).mean(-1, keepdims=True) + RMS_EPS) * g).astype(x.dtype)


def l2norm(x):
    return x * lax.rsqrt((x * x).sum(-1, keepdims=True) + 1e-6)


def split_cols(z, sizes):
    out, start = [], 0
    for s in sizes:
        out.append(z[..., start:start + s])
        start += s
    return out


def rope(x, pos):
    half = x.shape[-1] // 2
    inv = ROPE_THETA ** (-jnp.arange(half, dtype=jnp.float32) / half)
    ang = pos.astype(jnp.float32)[:, None] * inv[None, :]
    ang = ang.reshape((ang.shape[0],) + (1,) * (x.ndim - 3) + (half,))
    cos, sin = jnp.cos(ang).astype(x.dtype), jnp.sin(ang).astype(x.dtype)
    x1, x2 = x[..., :half], x[..., half:]
    return jnp.concatenate([x1 * cos - x2 * sin, x2 * cos + x1 * sin], axis=-1)


def causal_dwconv(xpad, w):
    return lax.conv_general_dilated(xpad, w[:, None, :].astype(xpad.dtype), window_strides=(1,),
                                    padding='VALID', dimension_numbers=('NWC', 'WIO', 'NWC'),
                                    feature_group_count=xpad.shape[-1])


def mla_attend(qn, qr, kn, kr, v, mask):
    s = (jnp.einsum('bqhd,bkhd->bhqk', qn, kn) + jnp.einsum('bqhr,bkr->bhqk', qr, kr)).astype(jnp.float32) * MLA_SCALE
    if mask is not None:
        s = jnp.where(mask, s, -jnp.inf)
    p = jax.nn.softmax(s, axis=-1).astype(v.dtype)
    return jnp.einsum('bhqk,bkhd->bqhd', p, v)


def mla_prompt_attention(qn, qr, kn, kr, v, pos):
    B, L, H, _ = qn.shape
    nb = L // QBLOCK
    kchunk = pos // CHUNK
    qn_b = jnp.moveaxis(qn.reshape(B, nb, QBLOCK, H, MLA_NOPE), 1, 0)
    qr_b = jnp.moveaxis(qr.reshape(B, nb, QBLOCK, H, MLA_ROPE), 1, 0)
    qc_b = kchunk.reshape(nb, QBLOCK)

    def one_block(args):
        qn_i, qr_i, qc_i = args
        mask = kchunk[None, :] <= qc_i[:, None]
        return mla_attend(qn_i, qr_i, kn, kr, v, mask)

    o = lax.map(one_block, (qn_b, qr_b, qc_b))
    return jnp.moveaxis(o, 0, 1).reshape(B, L, H, MLA_V)


def gated_delta_chunked(q, k, v, g, beta, s0):
    f32 = jnp.float32
    B, L, H, DK = q.shape
    DV = v.shape[-1]
    C = min(CHUNK, L)
    N = L // C
    q = l2norm(q.astype(f32)) * (DK ** -0.5)
    k = l2norm(k.astype(f32))

    def blocks(t):
        t = t.reshape((B, N, C, H) + t.shape[3:])
        return jnp.moveaxis(jnp.moveaxis(t, 1, 0), 3, 2)

    qb, kb, vb = blocks(q), blocks(k), blocks(v.astype(f32))
    gb = jnp.cumsum(blocks(g.astype(f32)), axis=-1)
    bb = blocks(beta.astype(f32))
    idx = jnp.arange(C)
    causal = idx[:, None] >= idx[None, :]
    strict = idx[:, None] > idx[None, :]
    decay = jnp.exp(jnp.where(causal, gb[..., :, None] - gb[..., None, :], -jnp.inf))
    kbeta = kb * bb[..., None]
    m = jnp.where(strict, jnp.einsum('nbhid,nbhjd->nbhij', kbeta, kb) * decay, 0.0)
    a = m + jnp.eye(C, dtype=f32)
    rhs = jnp.concatenate([vb * bb[..., None], kbeta * jnp.exp(gb)[..., None]], axis=-1)
    sol = lax.linalg.triangular_solve(a, rhs, left_side=True, lower=True, unit_diagonal=True)
    u, w = sol[..., :DV], sol[..., DV:]
    qk = jnp.einsum('nbhid,nbhjd->nbhij', qb, kb) * decay

    def step(s, inp):
        q_c, k_c, u_c, w_c, g_c, qk_c = inp
        v_new = u_c - jnp.einsum('bhck,bhkv->bhcv', w_c, s)
        o = (jnp.einsum('bhck,bhkv->bhcv', q_c * jnp.exp(g_c)[..., None], s)
             + jnp.einsum('bhij,bhjv->bhiv', qk_c, v_new))
        g_last = g_c[..., -1:]
        s = (s * jnp.exp(g_last)[..., None]
             + jnp.einsum('bhck,bhcv->bhkv', k_c * jnp.exp(g_last - g_c)[..., None], v_new))
        return s, o

    s, o = lax.scan(step, s0.astype(f32), (qb, kb, u, w, gb, qk))
    o = jnp.moveaxis(jnp.moveaxis(o, 2, 3), 0, 1).reshape(B, L, H, DV)
    return o.astype(v.dtype), s.astype(s0.dtype)


def token_mixer(h, pos, past, mix):
    ckv_past, kr_past, cm_state, gc_state, gdn_state = past
    (w_in, b_gate, q_norm, w_uq, kv_norm, w_ukv, w_pa, cm_w_dw, cm_b_dw, cm_ln_g, cm_ln_b, w_pb,
     g_w_conv, g_a_log, g_dt_bias, g_norm, w_pc, w_out) = mix
    f32 = jnp.float32
    B, L, _ = h.shape
    z = h @ w_in
    gate_pre, c_q, c_kv, k_r, glu_in, qkv, g_z, g_a, g_b = split_cols(z, IN_SIZES)

    q = (rms_norm(c_q, q_norm) @ w_uq).reshape(B, L, MLA_HEADS, MLA_NOPE + MLA_ROPE)
    q_nope, q_rope = q[..., :MLA_NOPE], rope(q[..., MLA_NOPE:], pos)
    ckv = rms_norm(c_kv, kv_norm)
    krope = rope(k_r, pos)
    if ckv_past is None:
        ckv_all, kr_all = ckv, krope
    else:
        ckv_all = jnp.concatenate([ckv_past, ckv], axis=1)
        kr_all = jnp.concatenate([kr_past, krope], axis=1)
    kv = (ckv_all @ w_ukv).reshape(B, -1, MLA_HEADS, MLA_NOPE + MLA_V)
    k_nope, v = kv[..., :MLA_NOPE], kv[..., MLA_NOPE:]
    if ckv_past is None:
        o_a = mla_prompt_attention(q_nope, q_rope, k_nope, kr_all, v, pos)
    else:
        o_a = mla_attend(q_nope, q_rope, k_nope, kr_all, v, None)
    y_a = o_a.reshape(B, L, MLA_HEADS * MLA_V) @ w_pa

    glu = glu_in[..., :CM_CH] * jax.nn.sigmoid(glu_in[..., CM_CH:])
    cm_pad = jnp.concatenate([cm_state.astype(glu.dtype), glu], axis=1)
    cm_new = cm_pad[:, -(CM_WIDTH - 1):]
    u = causal_dwconv(cm_pad, cm_w_dw) + cm_b_dw
    y_b = jax.nn.silu(layer_norm(u, cm_ln_g, cm_ln_b)) @ w_pb

    g_pad = jnp.concatenate([gc_state.astype(qkv.dtype), qkv], axis=1)
    gc_new = g_pad[:, -(GDN_CONV - 1):]
    qkv_c = jax.nn.silu(causal_dwconv(g_pad, g_w_conv))
    dq = GDN_HEADS * GDN_DK
    gq = qkv_c[..., :dq].reshape(B, L, GDN_HEADS, GDN_DK)
    gk = qkv_c[..., dq:2 * dq].reshape(B, L, GDN_HEADS, GDN_DK)
    gv = qkv_c[..., 2 * dq:].reshape(B, L, GDN_HEADS, GDN_DV)
    beta = jax.nn.sigmoid(g_b.astype(f32))
    log_decay = -jnp.exp(g_a_log.astype(f32)) * jax.nn.softplus(g_a.astype(f32) + g_dt_bias.astype(f32))
    o_c, s_new = gated_delta_chunked(gq, gk, gv, log_decay, beta, gdn_state)
    o_c = rms_norm(o_c, g_norm) * jax.nn.silu(g_z.reshape(B, L, GDN_HEADS, GDN_DV))
    y_c = o_c.reshape(B, L, GDN_HEADS * GDN_DV) @ w_pc

    gates = jax.nn.sigmoid(gate_pre.reshape(B, L, N_BRANCH, D_MODEL) + b_gate)
    merged = gates[..., 0, :] * y_a + gates[..., 1, :] * y_b + gates[..., 2, :] * y_c
    return merged @ w_out, (ckv, krope, cm_new, gc_new, s_new)


def cross_attention(h, mem_k, mem_v, w_q, w_o):
    B, L, _ = h.shape
    q = (h @ w_q).reshape(B, L, XA_HEADS, XA_DIM)
    s = jnp.einsum('blhd,bmhd->bhlm', q, mem_k).astype(jnp.float32) * (XA_DIM ** -0.5)
    p = jax.nn.softmax(s, axis=-1).astype(h.dtype)
    o = jnp.einsum('bhlm,bmhd->blhd', p, mem_v).reshape(B, L, D_MODEL)
    return o @ w_o


def hier_moe(h, w_rg, b_rg, w_re, b_re, w_gate, w_up, w_down):
    f32 = jnp.float32
    B, L, D = h.shape
    x = h.reshape(B * L, D)
    glog = (x @ w_rg + b_rg).astype(f32)
    gprob = jax.nn.softmax(glog, axis=-1)
    gsel = jnp.argmax(glog, axis=-1)
    gw = jnp.take_along_axis(gprob, gsel[:, None], axis=-1)
    elog = (x @ w_re + b_re).astype(f32).reshape(-1, MOE_GROUPS, MOE_PER_GROUP)
    elog = jnp.take_along_axis(elog, gsel[:, None, None], axis=1)[:, 0]
    tv, ti = lax.top_k(elog, MOE_TOPK)
    ew = jax.nn.softmax(tv, axis=-1) * gw
    eid = gsel[:, None] * MOE_PER_GROUP + ti
    combine = jnp.einsum('tk,tke->te', ew, jax.nn.one_hot(eid, MOE_EXPERTS, dtype=f32)).astype(h.dtype)
    a = jnp.einsum('td,edf->tef', x, w_gate)
    b = jnp.einsum('td,edf->tef', x, w_up)
    hid = jax.nn.silu(a) * b * combine[..., None]
    y = jnp.einsum('tef,efd->td', hid, w_down)
    return y.reshape(B, L, D)


def layer(h, pos, past, mem_k, mem_v, mix, xa, moe, norms):
    ln1_g, ln1_b, ln2_g, ln2_b, ln3_g, ln3_b = norms
    t, new_state = token_mixer(h, pos, past, mix)
    h = layer_norm(DN_ALPHA * h + t, ln1_g, ln1_b)
    h = layer_norm(DN_ALPHA * h + cross_attention(h, mem_k, mem_v, xa[0], xa[1]), ln2_g, ln2_b)
    h = layer_norm(DN_ALPHA * h + hier_moe(h, *moe), ln3_g, ln3_b)
    return h, new_state


def setup_inputs(seed: int = 0) -> dict:
    key = jax.random.key(seed)
    keys = jax.random.split(key, 96)
    counter = [0]

    def nk():
        k = keys[counter[0]]
        counter[0] += 1
        return k

    def nrm(shape, scale=1.0):
        return jax.random.normal(nk(), shape, jnp.float32) * scale

    def gain(shape):
        return 1.0 + nrm(shape, 0.02)

    Lc, D = DEPTH, D_MODEL
    dt = jnp.exp(jax.random.uniform(nk(), (Lc, GDN_HEADS), jnp.float32, math.log(1e-3), math.log(1e-1)))
    inp = {}
    inp['x_prompt'] = nrm((BATCH, SEQ, D))
    inp['x_sample'] = nrm((DEC_BATCH, DEC_SEQ, D))
    inp['mem_prompt'] = nrm((BATCH, MEM_LEN, D))
    inp['cache_mla_ckv'] = nrm((Lc, DEC_BATCH, PAST_LEN, MLA_KV_LORA))
    inp['cache_mla_krope'] = nrm((Lc, DEC_BATCH, PAST_LEN, MLA_ROPE))
    inp['state_cm_conv'] = nrm((Lc, DEC_BATCH, CM_WIDTH - 1, CM_CH), 0.5)
    inp['state_gdn_conv'] = nrm((Lc, DEC_BATCH, GDN_CONV - 1, GDN_QKV))
    inp['state_gdn'] = nrm((Lc, DEC_BATCH, GDN_HEADS, GDN_DK, GDN_DV), 0.1)
    inp['cache_mem_k'] = nrm((Lc, DEC_BATCH, MEM_LEN, XA_HEADS, XA_DIM))
    inp['cache_mem_v'] = nrm((Lc, DEC_BATCH, MEM_LEN, XA_HEADS, XA_DIM))
    inp['ln_in_g'] = gain((D,))
    inp['ln_in_b'] = nrm((D,), 0.02)
    inp['w_in'] = nrm((Lc, D, IN_COLS), D ** -0.5)
    inp['b_gate'] = nrm((Lc, N_BRANCH, D), 0.02)
    inp['mla_q_norm'] = gain((Lc, MLA_Q_LORA))
    inp['mla_w_uq'] = nrm((Lc, MLA_Q_LORA, MLA_HEADS * (MLA_NOPE + MLA_ROPE)), MLA_Q_LORA ** -0.5)
    inp['mla_kv_norm'] = gain((Lc, MLA_KV_LORA))
    inp['mla_w_ukv'] = nrm((Lc, MLA_KV_LORA, MLA_HEADS * (MLA_NOPE + MLA_V)), MLA_KV_LORA ** -0.5)
    inp['w_proj_a'] = nrm((Lc, MLA_HEADS * MLA_V, D), (MLA_HEADS * MLA_V) ** -0.5)
    inp['cm_w_dw'] = nrm((Lc, CM_WIDTH, CM_CH), CM_WIDTH ** -0.5)
    inp['cm_b_dw'] = nrm((Lc, CM_CH), 0.02)
    inp['cm_ln_g'] = gain((Lc, CM_CH))
    inp['cm_ln_b'] = nrm((Lc, CM_CH), 0.02)
    inp['w_proj_b'] = nrm((Lc, CM_CH, D), CM_CH ** -0.5)
    inp['gdn_w_conv'] = nrm((Lc, GDN_CONV, GDN_QKV), GDN_CONV ** -0.5)
    inp['gdn_a_log'] = jnp.log(jax.random.uniform(nk(), (Lc, GDN_HEADS), jnp.float32, 1.0, 16.0))
    inp['gdn_dt_bias'] = dt + jnp.log(-jnp.expm1(-dt))
    inp['gdn_norm'] = gain((Lc, GDN_DV))
    inp['w_proj_c'] = nrm((Lc, GDN_HEADS * GDN_DV, D), (GDN_HEADS * GDN_DV) ** -0.5)
    inp['w_out'] = nrm((Lc, D, D), D ** -0.5 * DN_BETA)
    inp['ln1_g'] = gain((Lc, D))
    inp['ln1_b'] = nrm((Lc, D), 0.02)
    inp['xa_w_q'] = nrm((Lc, D, D), D ** -0.5)
    inp['xa_w_k'] = nrm((Lc, D, D), D ** -0.5)
    inp['xa_w_v'] = nrm((Lc, D, D), D ** -0.5)
    inp['xa_w_o'] = nrm((Lc, D, D), D ** -0.5 * DN_BETA)
    inp['ln2_g'] = gain((Lc, D))
    inp['ln2_b'] = nrm((Lc, D), 0.02)
    inp['moe_w_rg'] = nrm((Lc, D, MOE_GROUPS), D ** -0.5)
    inp['moe_b_rg'] = nrm((Lc, MOE_GROUPS), 0.01)
    inp['moe_w_re'] = nrm((Lc, D, MOE_EXPERTS), D ** -0.5)
    inp['moe_b_re'] = nrm((Lc, MOE_EXPERTS), 0.01)
    inp['moe_w_gate'] = nrm((Lc, MOE_EXPERTS, D, MOE_FF), D ** -0.5)
    inp['moe_w_up'] = nrm((Lc, MOE_EXPERTS, D, MOE_FF), D ** -0.5)
    inp['moe_w_down'] = nrm((Lc, MOE_EXPERTS, MOE_FF, D), MOE_FF ** -0.5 * DN_BETA)
    inp['ln3_g'] = gain((Lc, D))
    inp['ln3_b'] = nrm((Lc, D), 0.02)
    return inp


def reference(x_prompt, x_sample, mem_prompt, cache_mla_ckv, cache_mla_krope, state_cm_conv,
              state_gdn_conv, state_gdn, cache_mem_k, cache_mem_v, ln_in_g, ln_in_b, w_in, b_gate,
              mla_q_norm, mla_w_uq, mla_kv_norm, mla_w_ukv, w_proj_a, cm_w_dw, cm_b_dw, cm_ln_g,
              cm_ln_b, w_proj_b, gdn_w_conv, gdn_a_log, gdn_dt_bias, gdn_norm, w_proj_c, w_out,
              ln1_g, ln1_b, xa_w_q, xa_w_k, xa_w_v, xa_w_o, ln2_g, ln2_b, moe_w_rg, moe_b_rg,
              moe_w_re, moe_b_re, moe_w_gate, moe_w_up, moe_w_down, ln3_g, ln3_b):
    b_p, l_p, _ = x_prompt.shape
    b_s, l_s, _ = x_sample.shape
    past = cache_mla_ckv.shape[2]
    pos_p = jnp.arange(l_p)
    pos_s = past + jnp.arange(l_s)
    dt = x_prompt.dtype
    cm0 = jnp.zeros((b_p, CM_WIDTH - 1, CM_CH), dt)
    gc0 = jnp.zeros((b_p, GDN_CONV - 1, GDN_QKV), dt)
    s0 = jnp.zeros((b_p, GDN_HEADS, GDN_DK, GDN_DV), dt)
    hp = layer_norm(x_prompt, ln_in_g, ln_in_b)
    hs = layer_norm(x_sample, ln_in_g, ln_in_b)
    outs_p = [[] for _ in range(7)]
    outs_s = [[] for _ in range(5)]
    for l in range(DEPTH):
        mix = (w_in[l], b_gate[l], mla_q_norm[l], mla_w_uq[l], mla_kv_norm[l], mla_w_ukv[l],
               w_proj_a[l], cm_w_dw[l], cm_b_dw[l], cm_ln_g[l], cm_ln_b[l], w_proj_b[l],
               gdn_w_conv[l], gdn_a_log[l], gdn_dt_bias[l], gdn_norm[l], w_proj_c[l], w_out[l])
        xa = (xa_w_q[l], xa_w_o[l])
        moe = (moe_w_rg[l], moe_b_rg[l], moe_w_re[l], moe_b_re[l], moe_w_gate[l], moe_w_up[l], moe_w_down[l])
        norms = (ln1_g[l], ln1_b[l], ln2_g[l], ln2_b[l], ln3_g[l], ln3_b[l])
        mk = (mem_prompt @ xa_w_k[l]).reshape(b_p, -1, XA_HEADS, XA_DIM)
        mv = (mem_prompt @ xa_w_v[l]).reshape(b_p, -1, XA_HEADS, XA_DIM)
        hp, st_p = layer(hp, pos_p, (None, None, cm0, gc0, s0), mk, mv, mix, xa, moe, norms)
        hs, st_s = layer(hs, pos_s, (cache_mla_ckv[l], cache_mla_krope[l], state_cm_conv[l],
                                     state_gdn_conv[l], state_gdn[l]),
                         cache_mem_k[l], cache_mem_v[l], mix, xa, moe, norms)
        for lst, arr in zip(outs_p, st_p + (mk, mv)):
            lst.append(arr)
        for lst, arr in zip(outs_s, st_s):
            lst.append(arr)
    ckv_p, kr_p, cm_p, gc_p, gdn_p, mk_p, mv_p = [jnp.stack(a) for a in outs_p]
    ckv_s, kr_s, cm_s, gc_s, gdn_s = [jnp.stack(a) for a in outs_s]
    return (hp, hs, ckv_p, kr_p, ckv_s, kr_s, cm_p, cm_s, gc_p, gc_s, gdn_p, gdn_s, mk_p, mv_p)
```

```python
import functools

import jax
import jax.numpy as jnp
from jax import lax
from jax.experimental import pallas as pl
from jax.experimental.pallas import tpu as pltpu

F32 = jnp.float32
BF16 = jnp.bfloat16
HIGHEST = lax.Precision.HIGHEST

D_MODEL = 1024
CHUNK = 64
MLA_HEADS = 8
MLA_NOPE = 64
MLA_ROPE = 32
MLA_V = 64
MLA_Q_LORA = 384
MLA_KV_LORA = 256
MLA_SCALE = (MLA_NOPE + MLA_ROPE) ** -0.5
ROPE_THETA = 10000.0
CM_CH = 512
CM_WIDTH = 31
GDN_HEADS = 4
GDN_DK = 128
GDN_DV = 128
GDN_CONV = 4
GDN_QKV = GDN_HEADS * (2 * GDN_DK + GDN_DV)
XA_HEADS = 4
XA_DIM = D_MODEL // XA_HEADS
MOE_GROUPS = 4
MOE_PER_GROUP = 4
MOE_EXPERTS = MOE_GROUPS * MOE_PER_GROUP
MOE_FF = 256
N_BRANCH = 3
LN_EPS = 1e-5
RMS_EPS = 1e-6
IN_SIZES = (N_BRANCH * D_MODEL, MLA_Q_LORA, MLA_KV_LORA, MLA_ROPE, 2 * CM_CH,
            GDN_QKV, GDN_HEADS * GDN_DV, GDN_HEADS, GDN_HEADS)

LANE = 128
HEAD_SLOT = 128
ROPE_LANE0 = MLA_NOPE
OFF_GATE = 0
OFF_CQ = OFF_GATE + N_BRANCH * D_MODEL
OFF_CKV = OFF_CQ + MLA_Q_LORA
OFF_KR = OFF_CKV + MLA_KV_LORA
OFF_KRR = OFF_KR + LANE
OFF_GLU = OFF_KRR + LANE
OFF_QKV = OFF_GLU + 2 * CM_CH
OFF_GZ = OFF_QKV + GDN_QKV
OFF_GAB = OFF_GZ + GDN_HEADS * GDN_DV
IN_PACKED = OFF_GAB + LANE
CM_HALO = 32
GDN_HALO = 8
VMEM_LIMIT = 56 * 1024 * 1024


def _dot(a, b):
    return jnp.dot(a, b, preferred_element_type=F32)


def _dot_nt(a, b):
    return lax.dot_general(a, b, (((1,), (1,)), ((), ())), preferred_element_type=F32)


def _dot_hi(a, b):
    return jnp.dot(a, b, precision=HIGHEST, preferred_element_type=F32)


def _layer_norm(x, g, b):
    xc = x - jnp.mean(x, axis=-1, keepdims=True)
    var = jnp.mean(xc * xc, axis=-1, keepdims=True)
    return xc * lax.rsqrt(var + LN_EPS) * g + b


def _rms_norm(x, g):
    return x * lax.rsqrt(jnp.mean(x * x, axis=-1, keepdims=True) + RMS_EPS) * g


def _sigmoid(x):
    return 1.0 / (1.0 + jnp.exp(-x))


def _silu(x):
    return x * _sigmoid(x)


def _resident(shape):
    nd = len(shape)
    return pl.BlockSpec(shape, lambda *_: (0,) * nd, pipeline_mode=pl.Buffered(1))


def _params(sem, vmem=VMEM_LIMIT):
    return pltpu.CompilerParams(dimension_semantics=sem, vmem_limit_bytes=vmem)


def _inproj_kernel(x_ref, lng_ref, lnb_ref, w_ref, bg_ref, qn_ref, wq_ref, wqr_ref, kvn_ref,
                   invf_ref, *outs, pre_ln, seq_len, past, tm):
    if pre_ln:
        h_ref, outs = outs[0], outs[1:]
    gates_ref, qp_ref, ckv_ref, kr_ref, glu_ref, qkv_ref, gz_ref, gab_ref = outs
    x = x_ref[...]
    if pre_ln:
        x = _layer_norm(x, lng_ref[...], lnb_ref[...])
        h_ref[...] = x
    xb = x.astype(BF16)

    def grp(off, n):
        return _dot(xb, w_ref[:, off:off + n])

    for j in range(N_BRANCH):
        sl = slice(j * D_MODEL, (j + 1) * D_MODEL)
        gates_ref[:, sl] = _sigmoid(grp(OFF_GATE + j * D_MODEL, D_MODEL) + bg_ref[:, sl])

    row = pl.program_id(0) * tm + lax.broadcasted_iota(jnp.int32, (tm, 1), 0)
    pos = (row % seq_len + past).astype(F32)
    ang_q = pos * invf_ref[0:1, :]
    ang_k = pos * invf_ref[1:2, :]
    cos_q, sin_q = jnp.cos(ang_q) * MLA_SCALE, jnp.sin(ang_q) * MLA_SCALE
    cos_k, sin_k = jnp.cos(ang_k), jnp.sin(ang_k)

    qn = _rms_norm(grp(OFF_CQ, MLA_Q_LORA), qn_ref[...]).astype(BF16)
    for h in range(MLA_HEADS):
        sl = slice(h * HEAD_SLOT, (h + 1) * HEAD_SLOT)
        q1 = _dot(qn, wq_ref[:, sl])
        q2 = _dot(qn, wqr_ref[:, sl])
        qp_ref[:, sl] = (q1 * cos_q + q2 * sin_q).astype(BF16)

    ckv_ref[...] = _rms_norm(grp(OFF_CKV, MLA_KV_LORA), kvn_ref[...])
    kr = grp(OFF_KR, LANE) * cos_k + grp(OFF_KRR, LANE) * sin_k
    kr_ref[...] = kr[:, :MLA_ROPE]

    glu_ref[...] = grp(OFF_GLU, CM_CH) * _sigmoid(grp(OFF_GLU + CM_CH, CM_CH))
    for j in range(GDN_QKV // 512):
        qkv_ref[:, j * 512:(j + 1) * 512] = grp(OFF_QKV + j * 512, 512)
    gz_ref[...] = grp(OFF_GZ, GDN_HEADS * GDN_DV)
    gab_ref[...] = grp(OFF_GAB, LANE)


def _inproj(x, lp, *, pre_ln, seq_len, past, tm=256):
    t = x.shape[0]
    tm = min(tm, t)
    assert t % tm == 0
    row = lambda n: pl.BlockSpec((tm, n), lambda i: (i, 0))
    out_shape = [
        jax.ShapeDtypeStruct((t, N_BRANCH * D_MODEL), F32),
        jax.ShapeDtypeStruct((t, MLA_HEADS * HEAD_SLOT), BF16),
        jax.ShapeDtypeStruct((t, MLA_KV_LORA), F32),
        jax.ShapeDtypeStruct((t, MLA_ROPE), F32),
        jax.ShapeDtypeStruct((t, CM_CH), F32),
        jax.ShapeDtypeStruct((t, GDN_QKV), F32),
        jax.ShapeDtypeStruct((t, GDN_HEADS * GDN_DV), F32),
        jax.ShapeDtypeStruct((t, LANE), F32),
    ]
    out_specs = [row(s.shape[1]) for s in out_shape]
    if pre_ln:
        out_shape = [jax.ShapeDtypeStruct((t, D_MODEL), F32)] + out_shape
        out_specs = [row(D_MODEL)] + out_specs
    return pl.pallas_call(
        functools.partial(_inproj_kernel, pre_ln=pre_ln, seq_len=seq_len, past=past, tm=tm),
        grid=(t // tm,),
        in_specs=[row(D_MODEL), _resident((1, D_MODEL)), _resident((1, D_MODEL)),
                  _resident((D_MODEL, IN_PACKED)), _resident((1, N_BRANCH * D_MODEL)),
                  _resident((1, MLA_Q_LORA)), _resident((MLA_Q_LORA, MLA_HEADS * HEAD_SLOT)),
                  _resident((MLA_Q_LORA, MLA_HEADS * HEAD_SLOT)), _resident((1, MLA_KV_LORA)),
                  _resident((2, LANE))],
        out_specs=out_specs,
        out_shape=out_shape,
        compiler_params=_params(("parallel",)),
        name="inproj",
    )(x, lp["ln_in_g"], lp["ln_in_b"], lp["w_in"], lp["b_gate"], lp["q_norm"], lp["wq"], lp["wqr"],
      lp["kv_norm"], lp["invf"])


def _kvup_kernel(ckv_ref, kr_ref, wk_ref, pk_ref, wv_ref, k_ref, v_ref):
    c = ckv_ref[...].astype(BF16)
    r = kr_ref[...].astype(BF16)
    k_ref[...] = (_dot(c, wk_ref[...]) + _dot(r, pk_ref[...])).astype(BF16)
    v_ref[...] = _dot(c, wv_ref[...]).astype(BF16)


def _kvup(ckv, kr, lp, tm=512):
    t = ckv.shape[0]
    tm = min(tm, t)
    assert t % tm == 0
    row = lambda n: pl.BlockSpec((tm, n), lambda i: (i, 0))
    return pl.pallas_call(
        _kvup_kernel,
        grid=(t // tm,),
        in_specs=[row(MLA_KV_LORA), row(MLA_ROPE), _resident((MLA_KV_LORA, MLA_HEADS * HEAD_SLOT)),
                  _resident((MLA_ROPE, MLA_HEADS * HEAD_SLOT)), _resident((MLA_KV_LORA, MLA_HEADS * MLA_V))],
        out_specs=[row(MLA_HEADS * HEAD_SLOT), row(MLA_HEADS * MLA_V)],
        out_shape=[jax.ShapeDtypeStruct((t, MLA_HEADS * HEAD_SLOT), BF16),
                   jax.ShapeDtypeStruct((t, MLA_HEADS * MLA_V), BF16)],
        compiler_params=_params(("parallel",)),
        name="kvup",
    )(ckv, kr, lp["wk"], lp["pk"], lp["wv"])


def _attn_prompt_kernel(q_ref, k_ref, v_ref, o_ref, *, tq):
    i = pl.program_id(2)
    rc = lax.broadcasted_iota(jnp.int32, (tq, tq), 0) // CHUNK
    cc = lax.broadcasted_iota(jnp.int32, (tq, tq), 1) // CHUNK
    visible = cc <= rc
    qs = [q_ref[:, j * HEAD_SLOT:(j + 1) * HEAD_SLOT] for j in range(2)]

    def scores(j, start):
        kb = k_ref[pl.ds(start, tq), j * HEAD_SLOT:(j + 1) * HEAD_SLOT]
        return _dot_nt(qs[j], kb)

    d0 = pl.multiple_of(i * tq, tq)
    vd = v_ref[pl.ds(d0, tq), :]
    carry = []
    for j in range(2):
        s = jnp.where(visible, scores(j, d0), -jnp.inf)
        m = jnp.max(s, axis=-1, keepdims=True)
        p = jnp.exp(s - m)
        carry += [m, jnp.sum(p, axis=-1, keepdims=True), _dot(p.astype(BF16), vd)]

    def body(kb, c):
        start = pl.multiple_of(kb * tq, tq)
        vb = v_ref[pl.ds(start, tq), :]
        out = []
        for j in range(2):
            m, l, acc = c[3 * j:3 * j + 3]
            s = scores(j, start)
            m_new = jnp.maximum(m, jnp.max(s, axis=-1, keepdims=True))
            alpha = jnp.exp(m - m_new)
            p = jnp.exp(s - m_new)
            out += [m_new, alpha * l + jnp.sum(p, axis=-1, keepdims=True),
                    alpha * acc + _dot(p.astype(BF16), vb)]
        return tuple(out)

    c = lax.fori_loop(0, i, body, tuple(carry))
    lane = lax.broadcasted_iota(jnp.int32, (tq, 2 * MLA_V), 1)
    o = jnp.where(lane < MLA_V, c[2] / c[1], c[5] / c[4])
    o_ref[...] = o.astype(BF16)


def _attn_prompt(qp, kp, vp, batch, seq_len, tq=256):
    assert seq_len % tq == 0 and tq % CHUNK == 0
    nq = seq_len // tq
    t = batch * seq_len
    return pl.pallas_call(
        functools.partial(_attn_prompt_kernel, tq=tq),
        grid=(batch, MLA_HEADS // 2, nq),
        in_specs=[pl.BlockSpec((tq, 2 * HEAD_SLOT), lambda b, hp, i: (b * nq + i, hp)),
                  pl.BlockSpec((seq_len, 2 * HEAD_SLOT), lambda b, hp, i: (b, hp)),
                  pl.BlockSpec((seq_len, 2 * MLA_V), lambda b, hp, i: (b, hp))],
        out_specs=pl.BlockSpec((tq, 2 * MLA_V), lambda b, hp, i: (b * nq + i, hp)),
        out_shape=jax.ShapeDtypeStruct((t, MLA_HEADS * MLA_V), BF16),
        compiler_params=_params(("parallel", "parallel", "arbitrary")),
        name="attn_prompt",
    )(qp, kp, vp)


def _attn_sample_kernel(q_ref, kp_ref, kn_ref, vp_ref, vn_ref, o_ref):
    vp, vn = vp_ref[...], vn_ref[...]
    outs = []
    for j in range(2):
        sl = slice(j * HEAD_SLOT, (j + 1) * HEAD_SLOT)
        q = q_ref[:, sl]
        sp = _dot_nt(q, kp_ref[:, sl])
        sn = _dot_nt(q, kn_ref[:, sl])
        m = jnp.maximum(jnp.max(sp, axis=-1, keepdims=True), jnp.max(sn, axis=-1, keepdims=True))
        pp, pn = jnp.exp(sp - m), jnp.exp(sn - m)
        l = jnp.sum(pp, axis=-1, keepdims=True) + jnp.sum(pn, axis=-1, keepdims=True)
        outs.append((_dot(pp.astype(BF16), vp) + _dot(pn.astype(BF16), vn)) / l)
    lane = lax.broadcasted_iota(jnp.int32, outs[0].shape, 1)
    o_ref[...] = jnp.where(lane < MLA_V, outs[0], outs[1]).astype(BF16)


def _attn_sample(qp, kpast, knew, vpast, vnew, batch, seq_len, past):
    blk = lambda rows, n: pl.BlockSpec((rows, n), lambda b, hp: (b, hp))
    return pl.pallas_call(
        _attn_sample_kernel,
        grid=(batch, MLA_HEADS // 2),
        in_specs=[blk(seq_len, 2 * HEAD_SLOT), blk(past, 2 * HEAD_SLOT), blk(seq_len, 2 * HEAD_SLOT),
                  blk(past, 2 * MLA_V), blk(seq_len, 2 * MLA_V)],
        out_specs=blk(seq_len, 2 * MLA_V),
        out_shape=jax.ShapeDtypeStruct((batch * seq_len, MLA_HEADS * MLA_V), BF16),
        compiler_params=_params(("parallel", "parallel")),
        name="attn_sample",
    )(qp, kpast, knew, vpast, vnew)


def _conformer_kernel(x_ref, st_ref, w_ref, b_ref, g_ref, beta_ref, o_ref, xbuf, *, tl, sub):
    @pl.when(pl.program_id(1) == 0)
    def _():
        xbuf[0:CM_HALO, :] = st_ref[0]

    xbuf[CM_HALO:CM_HALO + tl, :] = x_ref[...]
    first = CM_HALO - (CM_WIDTH - 1)
    for r in range(tl // sub):
        acc = jnp.zeros((sub, CM_CH), F32) + b_ref[...]
        for j in range(CM_WIDTH):
            acc = acc + w_ref[j:j + 1, :] * xbuf[r * sub + first + j:r * sub + first + j + sub, :]
        y = _layer_norm(acc, g_ref[...], beta_ref[...])
        o_ref[r * sub:(r + 1) * sub, :] = _silu(y).astype(BF16)
    xbuf[0:CM_HALO, :] = xbuf[tl:tl + CM_HALO, :]


def _conformer(glu, state, lp, batch, seq_len):
    tl = min(seq_len, 256)
    assert seq_len % tl == 0 and tl >= CM_HALO
    nl = seq_len // tl
    return pl.pallas_call(
        functools.partial(_conformer_kernel, tl=tl, sub=32),
        grid=(batch, nl),
        in_specs=[pl.BlockSpec((tl, CM_CH), lambda b, l: (b * nl + l, 0)),
                  pl.BlockSpec((1, CM_HALO, CM_CH), lambda b, l: (b, 0, 0)),
                  _resident((CM_WIDTH, CM_CH)), _resident((1, CM_CH)), _resident((1, CM_CH)),
                  _resident((1, CM_CH))],
        out_specs=pl.BlockSpec((tl, CM_CH), lambda b, l: (b * nl + l, 0)),
        out_shape=jax.ShapeDtypeStruct((batch * seq_len, CM_CH), BF16),
        scratch_shapes=[pltpu.VMEM((CM_HALO + tl, CM_CH), F32)],
        compiler_params=_params(("parallel", "arbitrary")),
        name="conformer",
    )(glu, state, lp["cm_w_dw"], lp["cm_b_dw"], lp["cm_ln_g"], lp["cm_ln_b"])


def _unit_lower_inverse(a, eye):
    x = eye - a
    p = a
    n = a.shape[0]
    k = 2
    while k < n:
        p = _dot_hi(p, p)
        x = x + _dot_hi(x, p)
        k *= 2
    return x


def _gdn_kernel(qkv_ref, gab_ref, gz_ref, cst_ref, s0_ref, wc_ref, alog_ref, dtb_ref, gn_ref,
                o_ref, s_ref, xbuf, xc, *, rows):
    @pl.when(pl.program_id(1) == 0)
    def _():
        xbuf[0:GDN_HALO, :] = cst_ref[0]
        s_ref[...] = s0_ref[...]

    xbuf[GDN_HALO:GDN_HALO + rows, :] = qkv_ref[...]
    for c in range(GDN_QKV // LANE):
        sl = slice(c * LANE, (c + 1) * LANE)
        acc = wc_ref[GDN_CONV - 1:GDN_CONV, sl] * xbuf[GDN_HALO:GDN_HALO + rows, sl]
        for j in range(GDN_CONV - 1):
            lo = GDN_HALO - (GDN_CONV - 1) + j
            acc = acc + wc_ref[j:j + 1, sl] * xbuf[lo:lo + rows, sl]
        xc[:, sl] = _silu(acc)
    xbuf[0:GDN_HALO, :] = xbuf[rows:rows + GDN_HALO, :]

    gab = gab_ref[...]
    z = gab + dtb_ref[...]
    softplus = jnp.maximum(z, 0.0) + jnp.log(1.0 + jnp.exp(-jnp.abs(z)))
    g = -jnp.exp(alog_ref[...]) * softplus
    beta_all = _sigmoid(gab)
    ri = lax.broadcasted_iota(jnp.int32, (rows, rows), 0)
    ci = lax.broadcasted_iota(jnp.int32, (rows, rows), 1)
    tri = jnp.where((ri >= ci) & (ri // CHUNK == ci // CHUNK), 1.0, 0.0).astype(F32)
    gcs = _dot_hi(tri, g)
    gcs_t = gcs.T

    i64 = lax.broadcasted_iota(jnp.int32, (CHUNK, CHUNK), 0)
    j64 = lax.broadcasted_iota(jnp.int32, (CHUNK, CHUNK), 1)
    causal = i64 >= j64
    strict = i64 > j64
    eye = jnp.where(i64 == j64, 1.0, 0.0).astype(F32)
    dq, dk = GDN_HEADS * GDN_DK, GDN_HEADS * GDN_DK

    for c in range(rows // CHUNK):
        rs = slice(c * CHUNK, (c + 1) * CHUNK)
        for h in range(GDN_HEADS):
            q = xc[rs, h * GDN_DK:(h + 1) * GDN_DK]
            k = xc[rs, dq + h * GDN_DK:dq + (h + 1) * GDN_DK]
            v = xc[rs, dq + dk + h * GDN_DV:dq + dk + (h + 1) * GDN_DV]
            q = q * lax.rsqrt(jnp.sum(q * q, axis=-1, keepdims=True) + 1e-6) * (GDN_DK ** -0.5)
            k = k * lax.rsqrt(jnp.sum(k * k, axis=-1, keepdims=True) + 1e-6)
            gcol = gcs[rs, h:h + 1]
            grow = gcs_t[h:h + 1, rs]
            beta = beta_all[rs, GDN_HEADS + h:GDN_HEADS + h + 1]
            decay = jnp.exp(jnp.where(causal, gcol - grow, -jnp.inf))
            kbeta = k * beta
            a = jnp.where(strict, _dot_nt(kbeta, k) * decay, 0.0)
            tinv = _unit_lower_inverse(a, eye)
            egc = jnp.exp(gcol)
            u = _dot_hi(tinv, v * beta)
            w = _dot_hi(tinv, kbeta * egc)
            qk = _dot_nt(q, k) * decay
            s = s_ref[0, h]
            v_new = u - _dot(w, s)
            o = _dot(q * egc, s) + _dot(qk, v_new)
            glast = gcol[CHUNK - 1:CHUNK, :]
            kd = k * jnp.exp(glast - gcol)
            s_ref[0, h] = s * jnp.exp(glast) + lax.dot_general(
                kd, v_new, (((0,), (0,)), ((), ())), preferred_element_type=F32)
            gate = _silu(gz_ref[rs, h * GDN_DV:(h + 1) * GDN_DV])
            o_ref[rs, h * GDN_DV:(h + 1) * GDN_DV] = (_rms_norm(o, gn_ref[...]) * gate).astype(BF16)


def _gdn(qkv, gab, gz, conv_state, s0, lp, batch, seq_len):
    rows = min(seq_len, 256)
    assert seq_len % rows == 0 and rows % CHUNK == 0
    nl = seq_len // rows
    tile = lambda n: pl.BlockSpec((rows, n), lambda b, l: (b * nl + l, 0))
    return pl.pallas_call(
        functools.partial(_gdn_kernel, rows=rows),
        grid=(batch, nl),
        in_specs=[tile(GDN_QKV), tile(LANE), tile(GDN_HEADS * GDN_DV),
                  pl.BlockSpec((1, GDN_HALO, GDN_QKV), lambda b, l: (b, 0, 0)),
                  pl.BlockSpec((1, GDN_HEADS, GDN_DK, GDN_DV), lambda b, l: (b, 0, 0, 0)),
                  _resident((GDN_CONV, GDN_QKV)), _resident((1, LANE)), _resident((1, LANE)),
                  _resident((1, GDN_DV))],
        out_specs=[tile(GDN_HEADS * GDN_DV),
                   pl.BlockSpec((1, GDN_HEADS, GDN_DK, GDN_DV), lambda b, l: (b, 0, 0, 0))],
        out_shape=[jax.ShapeDtypeStruct((batch * seq_len, GDN_HEADS * GDN_DV), BF16),
                   jax.ShapeDtypeStruct((batch, GDN_HEADS, GDN_DK, GDN_DV), F32)],
        scratch_shapes=[pltpu.VMEM((GDN_HALO + rows, GDN_QKV), F32),
                        pltpu.VMEM((rows, GDN_QKV), F32)],
        compiler_params=_params(("parallel", "arbitrary")),
        name="gdn",
    )(qkv, gab, gz, conv_state, s0, lp["g_w_conv"], lp["g_a_log"], lp["g_dt_bias"], lp["g_norm"])


def _route(lg):
    lane = lax.broadcasted_iota(jnp.int32, lg.shape, 1)
    is_g = lane < MOE_GROUPS
    gl = jnp.where(is_g, lg, -jnp.inf)
    gmax = jnp.max(gl, axis=-1, keepdims=True)
    gsel = jnp.min(jnp.where(gl == gmax, lane, LANE), axis=-1, keepdims=True)
    gw = 1.0 / jnp.sum(jnp.where(is_g, jnp.exp(lg - gmax), 0.0), axis=-1, keepdims=True)
    e_lo = MOE_GROUPS + gsel * MOE_PER_GROUP
    in_grp = (lane >= e_lo) & (lane < e_lo + MOE_PER_GROUP)
    el = jnp.where(in_grp, lg, -jnp.inf)
    v1 = jnp.max(el, axis=-1, keepdims=True)
    i1 = jnp.min(jnp.where(el == v1, lane, LANE), axis=-1, keepdims=True)
    rest = in_grp & (lane != i1)
    el2 = jnp.where(rest, lg, -jnp.inf)
    v2 = jnp.max(el2, axis=-1, keepdims=True)
    i2 = jnp.min(jnp.where(rest & (el2 == v2), lane, LANE), axis=-1, keepdims=True)
    e2 = jnp.exp(v2 - v1)
    den = 1.0 + e2
    return jnp.where(lane == i1, gw / den, 0.0) + jnp.where(lane == i2, gw * e2 / den, 0.0)


def _merge_kernel(h_ref, g_ref, oa_ref, cb_ref, oc_ref, wpa_ref, wpb_ref, wpc_ref, wout_ref,
                  l1g_ref, l1b_ref, wq_ref, mk_ref, mv_ref, wo_ref, l2g_ref, l2b_ref, wr_ref, br_ref,
                  h2_ref, comb_ref, *, alpha):
    h = h_ref[...]
    merged = (g_ref[:, 0:D_MODEL] * _dot(oa_ref[...], wpa_ref[...])
              + g_ref[:, D_MODEL:2 * D_MODEL] * _dot(cb_ref[...], wpb_ref[...])
              + g_ref[:, 2 * D_MODEL:3 * D_MODEL] * _dot(oc_ref[...], wpc_ref[...]))
    t = _dot(merged.astype(BF16), wout_ref[...])
    h1 = _layer_norm(alpha * h + t, l1g_ref[...], l1b_ref[...])
    q = _dot(h1.astype(BF16), wq_ref[...]).astype(BF16)
    xo = jnp.zeros_like(h)
    for hh in range(XA_HEADS):
        sl = slice(hh * XA_DIM, (hh + 1) * XA_DIM)
        s = _dot_nt(q[:, sl], mk_ref[0, :, sl]) * (XA_DIM ** -0.5)
        p = jnp.exp(s - jnp.max(s, axis=-1, keepdims=True))
        p = p / jnp.sum(p, axis=-1, keepdims=True)
        o = _dot(p.astype(BF16), mv_ref[0, :, sl])
        xo = xo + _dot(o.astype(BF16), wo_ref[sl, :])
    h2 = _layer_norm(alpha * h1 + xo, l2g_ref[...], l2b_ref[...])
    h2_ref[...] = h2
    comb_ref[...] = _route(_dot(h2.astype(BF16), wr_ref[...]) + br_ref[...])


def _merge(h, gates, oa, cb, oc, mem_k, mem_v, lp, batch, seq_len, alpha):
    tm = min(seq_len, 256)
    assert seq_len % tm == 0
    per_b = seq_len // tm
    t = batch * seq_len
    mem_len = mem_k.shape[1]
    row = lambda n: pl.BlockSpec((tm, n), lambda i: (i, 0))
    mem = pl.BlockSpec((1, mem_len, D_MODEL), lambda i: (i // per_b, 0, 0))
    half = MLA_HEADS * MLA_V
    return pl.pallas_call(
        functools.partial(_merge_kernel, alpha=alpha),
        grid=(t // tm,),
        in_specs=[row(D_MODEL), row(N_BRANCH * D_MODEL), row(half), row(CM_CH), row(GDN_HEADS * GDN_DV),
                  _resident((half, D_MODEL)), _resident((CM_CH, D_MODEL)),
                  _resident((GDN_HEADS * GDN_DV, D_MODEL)), _resident((D_MODEL, D_MODEL)),
                  _resident((1, D_MODEL)), _resident((1, D_MODEL)), _resident((D_MODEL, D_MODEL)),
                  mem, mem, _resident((D_MODEL, D_MODEL)), _resident((1, D_MODEL)),
                  _resident((1, D_MODEL)), _resident((D_MODEL, LANE)), _resident((1, LANE))],
        out_specs=[row(D_MODEL), row(LANE)],
        out_shape=[jax.ShapeDtypeStruct((t, D_MODEL), F32), jax.ShapeDtypeStruct((t, LANE), F32)],
        compiler_params=_params(("parallel",)),
        name="merge",
    )(h, gates, oa, cb, oc, lp["w_pa"], lp["w_pb"], lp["w_pc"], lp["w_out"], lp["ln1_g"], lp["ln1_b"],
      lp["xa_wq"], mem_k, mem_v, lp["xa_wo"], lp["ln2_g"], lp["ln2_b"], lp["w_route"], lp["b_route"])


def _moe_kernel(x_ref, comb_ref, wg_ref, wu_ref, wd_ref, lg_ref, lb_ref, o_ref, xb_ref, acc_ref, *, alpha):
    e = pl.program_id(1)

    @pl.when(e == 0)
    def _():
        xb_ref[...] = x_ref[...].astype(BF16)
        acc_ref[...] = jnp.zeros_like(acc_ref)

    xb = xb_ref[...]
    lane = lax.broadcasted_iota(jnp.int32, comb_ref.shape, 1)
    cw = jnp.sum(jnp.where(lane == e + MOE_GROUPS, comb_ref[...], 0.0), axis=-1, keepdims=True)
    a = _dot(xb, wg_ref[0])
    b = _dot(xb, wu_ref[0])
    hid = _silu(a) * b * cw
    acc_ref[...] += _dot(hid.astype(BF16), wd_ref[0])

    @pl.when(e == MOE_EXPERTS - 1)
    def _():
        o_ref[...] = _layer_norm(alpha * x_ref[...] + acc_ref[...], lg_ref[...], lb_ref[...])


def _moe(x, comb, lp, alpha):
    t = x.shape[0]
    tm = min(t, 1024)
    assert t % tm == 0
    row = lambda n: pl.BlockSpec((tm, n), lambda i, e: (i, 0))
    return pl.pallas_call(
        functools.partial(_moe_kernel, alpha=alpha),
        grid=(t // tm, MOE_EXPERTS),
        in_specs=[row(D_MODEL), row(LANE),
                  pl.BlockSpec((1, D_MODEL, MOE_FF), lambda i, e: (e, 0, 0)),
                  pl.BlockSpec((1, D_MODEL, MOE_FF), lambda i, e: (e, 0, 0)),
                  pl.BlockSpec((1, MOE_FF, D_MODEL), lambda i, e: (e, 0, 0)),
                  pl.BlockSpec((1, D_MODEL), lambda i, e: (0, 0)),
                  pl.BlockSpec((1, D_MODEL), lambda i, e: (0, 0))],
        out_specs=row(D_MODEL),
        out_shape=jax.ShapeDtypeStruct((t, D_MODEL), F32),
        scratch_shapes=[pltpu.VMEM((tm, D_MODEL), BF16), pltpu.VMEM((tm, D_MODEL), F32)],
        compiler_params=_params(("parallel", "arbitrary")),
        name="moe",
    )(x, comb, lp["moe_wg"], lp["moe_wu"], lp["moe_wd"], lp["ln3_g"], lp["ln3_b"])


def _memkv_kernel(x_ref, wk_ref, wv_ref, k_ref, v_ref):
    xb = x_ref[...].astype(BF16)
    k_ref[...] = _dot(xb, wk_ref[...])
    v_ref[...] = _dot(xb, wv_ref[...])


def _memkv(mem, lp):
    t = mem.shape[0]
    full = lambda r, c: pl.BlockSpec((r, c), lambda i: (0, 0))
    return pl.pallas_call(
        _memkv_kernel,
        grid=(1,),
        in_specs=[full(t, D_MODEL), full(D_MODEL, D_MODEL), full(D_MODEL, D_MODEL)],
        out_specs=[full(t, D_MODEL), full(t, D_MODEL)],
        out_shape=[jax.ShapeDtypeStruct((t, D_MODEL), F32)] * 2,
        compiler_params=_params(("arbitrary",)),
        name="memkv",
    )(mem, lp["xa_wk"], lp["xa_wv"])


def _rot_half_cols(w):
    half = w.shape[-1] // 2
    return jnp.concatenate([-w[..., half:], w[..., :half]], axis=-1)


def _pack_layer(l, p):
    d = D_MODEL
    cols, start = [], 0
    for s in IN_SIZES:
        cols.append(p["w_in"][l][:, start:start + s])
        start += s
    w_gate, w_cq, w_ckv, w_kr, w_glu, w_qkv, w_gz, w_ga, w_gb = cols
    zpad = lambda n: jnp.zeros((d, n), F32)
    w_in = jnp.concatenate([
        w_gate, w_cq, w_ckv,
        w_kr, zpad(LANE - MLA_ROPE),
        _rot_half_cols(w_kr), zpad(LANE - MLA_ROPE),
        w_glu, w_qkv, w_gz,
        w_ga, w_gb, zpad(LANE - 2 * GDN_HEADS)], axis=1).astype(BF16)
    assert w_in.shape[1] == IN_PACKED

    wuq = p["mla_w_uq"][l].reshape(MLA_Q_LORA, MLA_HEADS, MLA_NOPE + MLA_ROPE)
    nope, rope = wuq[..., :MLA_NOPE], wuq[..., MLA_NOPE:]
    pad_q = HEAD_SLOT - MLA_NOPE - MLA_ROPE
    zq = lambda n: jnp.zeros((MLA_Q_LORA, MLA_HEADS, n), F32)
    wq = jnp.concatenate([nope, rope, zq(pad_q)], -1).reshape(MLA_Q_LORA, MLA_HEADS * HEAD_SLOT)
    wqr = jnp.concatenate([zq(MLA_NOPE), _rot_half_cols(rope), zq(pad_q)], -1).reshape(
        MLA_Q_LORA, MLA_HEADS * HEAD_SLOT)

    wukv = p["mla_w_ukv"][l].reshape(MLA_KV_LORA, MLA_HEADS, MLA_NOPE + MLA_V)
    wk = jnp.concatenate([wukv[..., :MLA_NOPE],
                          jnp.zeros((MLA_KV_LORA, MLA_HEADS, HEAD_SLOT - MLA_NOPE), F32)], -1)
    wk = wk.reshape(MLA_KV_LORA, MLA_HEADS * HEAD_SLOT)
    wv = wukv[..., MLA_NOPE:].reshape(MLA_KV_LORA, MLA_HEADS * MLA_V)
    place = jnp.concatenate([jnp.zeros((MLA_ROPE, ROPE_LANE0), F32), jnp.eye(MLA_ROPE, dtype=F32),
                             jnp.zeros((MLA_ROPE, pad_q), F32)], -1)
    pk = jnp.tile(place, (1, MLA_HEADS))

    half = MLA_ROPE // 2
    inv = ROPE_THETA ** (-jnp.arange(half, dtype=F32) / half)
    inv2 = jnp.concatenate([inv, inv])
    invf = jnp.stack([
        jnp.concatenate([jnp.zeros((ROPE_LANE0,), F32), inv2, jnp.zeros((pad_q,), F32)]),
        jnp.concatenate([inv2, jnp.zeros((LANE - MLA_ROPE,), F32)])])

    lane_pad = lambda v: jnp.concatenate([v, jnp.zeros((LANE - v.shape[0],), F32)])[None, :]
    w_route = jnp.concatenate([p["moe_w_rg"][l], p["moe_w_re"][l],
                               zpad(LANE - MOE_GROUPS - MOE_EXPERTS)], axis=1).astype(BF16)
    b_route = lane_pad(jnp.concatenate([p["moe_b_rg"][l], p["moe_b_re"][l]]))
    r1 = lambda v: v.reshape(1, -1)
    return dict(
        ln_in_g=r1(p["ln_in_g"]), ln_in_b=r1(p["ln_in_b"]),
        w_in=w_in, b_gate=r1(p["b_gate"][l]), q_norm=r1(p["mla_q_norm"][l]),
        wq=wq.astype(BF16), wqr=wqr.astype(BF16), kv_norm=r1(p["mla_kv_norm"][l]), invf=invf,
        wk=wk.astype(BF16), pk=pk.astype(BF16), wv=wv.astype(BF16),
        cm_w_dw=p["cm_w_dw"][l], cm_b_dw=r1(p["cm_b_dw"][l]), cm_ln_g=r1(p["cm_ln_g"][l]),
        cm_ln_b=r1(p["cm_ln_b"][l]),
        g_w_conv=p["gdn_w_conv"][l], g_a_log=lane_pad(p["gdn_a_log"][l]),
        g_dt_bias=lane_pad(p["gdn_dt_bias"][l]), g_norm=r1(p["gdn_norm"][l]),
        w_pa=p["w_proj_a"][l].astype(BF16), w_pb=p["w_proj_b"][l].astype(BF16),
        w_pc=p["w_proj_c"][l].astype(BF16), w_out=p["w_out"][l].astype(BF16),
        ln1_g=r1(p["ln1_g"][l]), ln1_b=r1(p["ln1_b"][l]),
        xa_wq=p["xa_w_q"][l].astype(BF16), xa_wk=p["xa_w_k"][l].astype(BF16),
        xa_wv=p["xa_w_v"][l].astype(BF16), xa_wo=p["xa_w_o"][l].astype(BF16),
        ln2_g=r1(p["ln2_g"][l]), ln2_b=r1(p["ln2_b"][l]),
        w_route=w_route, b_route=b_route,
        moe_wg=p["moe_w_gate"][l].astype(BF16), moe_wu=p["moe_w_up"][l].astype(BF16),
        moe_wd=p["moe_w_down"][l].astype(BF16),
        ln3_g=r1(p["ln3_g"][l]), ln3_b=r1(p["ln3_b"][l]),
    )


def _front_pad_rows(state, rows):
    b, r, c = state.shape
    return jnp.concatenate([jnp.zeros((b, rows - r, c), state.dtype), state], axis=1)


def _layer(x, lp, *, first, batch, seq_len, past, alpha, cache, cm_state, gc_state, s0, mem_k, mem_v):
    res = _inproj(x, lp, pre_ln=first, seq_len=seq_len, past=past)
    if first:
        h, res = res[0], res[1:]
    else:
        h = x
    gates, qp, ckv, kr, glu, qkv, gz, gab = res
    kn, vn = _kvup(ckv, kr, lp)
    if cache is None:
        oa = _attn_prompt(qp, kn, vn, batch, seq_len)
    else:
        ckv_past, kr_past = cache
        kpast, vpast = _kvup(ckv_past.reshape(batch * past, MLA_KV_LORA),
                             kr_past.reshape(batch * past, MLA_ROPE), lp)
        oa = _attn_sample(qp, kpast, kn, vpast, vn, batch, seq_len, past)
    cb = _conformer(glu, _front_pad_rows(cm_state, CM_HALO), lp, batch, seq_len)
    oc, s_new = _gdn(qkv, gab, gz, _front_pad_rows(gc_state, GDN_HALO), s0, lp, batch, seq_len)
    h2, comb = _merge(h, gates, oa, cb, oc, mem_k.astype(BF16), mem_v.astype(BF16), lp, batch, seq_len, alpha)
    h3 = _moe(h2, comb, lp, alpha)
    assert seq_len >= CM_WIDTH - 1
    states = (ckv.reshape(batch, seq_len, MLA_KV_LORA), kr.reshape(batch, seq_len, MLA_ROPE),
              glu.reshape(batch, seq_len, CM_CH)[:, seq_len - (CM_WIDTH - 1):],
              qkv.reshape(batch, seq_len, GDN_QKV)[:, seq_len - (GDN_CONV - 1):], s_new)
    return h3, states


def kernel(x_prompt, x_sample, mem_prompt, cache_mla_ckv, cache_mla_krope, state_cm_conv, state_gdn_conv, state_gdn, cache_mem_k, cache_mem_v, ln_in_g, ln_in_b, w_in, b_gate, mla_q_norm, mla_w_uq, mla_kv_norm, mla_w_ukv, w_proj_a, cm_w_dw, cm_b_dw, cm_ln_g, cm_ln_b, w_proj_b, gdn_w_conv, gdn_a_log, gdn_dt_bias, gdn_norm, w_proj_c, w_out, ln1_g, ln1_b, xa_w_q, xa_w_k, xa_w_v, xa_w_o, ln2_g, ln2_b, moe_w_rg, moe_b_rg, moe_w_re, moe_b_re, moe_w_gate, moe_w_up, moe_w_down, ln3_g, ln3_b):
    p = dict(ln_in_g=ln_in_g, ln_in_b=ln_in_b, w_in=w_in, b_gate=b_gate, mla_q_norm=mla_q_norm,
             mla_w_uq=mla_w_uq, mla_kv_norm=mla_kv_norm, mla_w_ukv=mla_w_ukv, w_proj_a=w_proj_a,
             cm_w_dw=cm_w_dw, cm_b_dw=cm_b_dw, cm_ln_g=cm_ln_g, cm_ln_b=cm_ln_b, w_proj_b=w_proj_b,
             gdn_w_conv=gdn_w_conv, gdn_a_log=gdn_a_log, gdn_dt_bias=gdn_dt_bias, gdn_norm=gdn_norm,
             w_proj_c=w_proj_c, w_out=w_out, ln1_g=ln1_g, ln1_b=ln1_b, xa_w_q=xa_w_q, xa_w_k=xa_w_k,
             xa_w_v=xa_w_v, xa_w_o=xa_w_o, ln2_g=ln2_g, ln2_b=ln2_b, moe_w_rg=moe_w_rg,
             moe_b_rg=moe_b_rg, moe_w_re=moe_w_re, moe_b_re=moe_b_re, moe_w_gate=moe_w_gate,
             moe_w_up=moe_w_up, moe_w_down=moe_w_down, ln3_g=ln3_g, ln3_b=ln3_b)
    depth = w_in.shape[0]
    alpha = (2 * depth) ** 0.25
    b_p, l_p, d = x_prompt.shape
    b_s, l_s, _ = x_sample.shape
    past = cache_mla_ckv.shape[2]
    mem_len = mem_prompt.shape[1]
    hp = x_prompt.reshape(b_p * l_p, d)
    hs = x_sample.reshape(b_s * l_s, d)
    cm0 = jnp.zeros((b_p, CM_WIDTH - 1, CM_CH), F32)
    gc0 = jnp.zeros((b_p, GDN_CONV - 1, GDN_QKV), F32)
    s0 = jnp.zeros((b_p, GDN_HEADS, GDN_DK, GDN_DV), F32)
    outs_p = [[] for _ in range(7)]
    outs_s = [[] for _ in range(5)]
    for l in range(depth):
        lp = _pack_layer(l, p)
        mk, mv = _memkv(mem_prompt.reshape(b_p * mem_len, d), lp)
        mk = mk.reshape(b_p, mem_len, d)
        mv = mv.reshape(b_p, mem_len, d)
        hp, st_p = _layer(hp, lp, first=(l == 0), batch=b_p, seq_len=l_p, past=0, alpha=alpha,
                          cache=None, cm_state=cm0, gc_state=gc0, s0=s0, mem_k=mk, mem_v=mv)
        hs, st_s = _layer(hs, lp, first=(l == 0), batch=b_s, seq_len=l_s, past=past, alpha=alpha,
                          cache=(cache_mla_ckv[l], cache_mla_krope[l]), cm_state=state_cm_conv[l],
                          gc_state=state_gdn_conv[l], s0=state_gdn[l],
                          mem_k=cache_mem_k[l].reshape(b_s, mem_len, d),
                          mem_v=cache_mem_v[l].reshape(b_s, mem_len, d))
        mk4 = mk.reshape(b_p, mem_len, XA_HEADS, XA_DIM)
        mv4 = mv.reshape(b_p, mem_len, XA_HEADS, XA_DIM)
        for lst, arr in zip(outs_p, st_p + (mk4, mv4)):
            lst.append(arr)
        for lst, arr in zip(outs_s, st_s):
            lst.append(arr)
    ckv_p, kr_p, cm_p, gc_p, gdn_p, mk_p, mv_p = [jnp.stack(a) for a in outs_p]
    ckv_s, kr_s, cm_s, gc_s, gdn_s = [jnp.stack(a) for a in outs_s]
    return (hp.reshape(b_p, l_p, d), hs.reshape(b_s, l_s, d), ckv_p, kr_p, ckv_s, kr_s,
            cm_p, cm_s, gc_p, gc_s, gdn_p, gdn_s, mk_p, mv_p)
```

```python
import functools

import jax
import jax.numpy as jnp
from jax import lax
from jax.experimental import pallas as pl
from jax.experimental.pallas import tpu as pltpu

F32 = jnp.float32
BF16 = jnp.bfloat16

D_MODEL = 1024
CHUNK = 64
MLA_HEADS = 8
MLA_NOPE = 64
MLA_ROPE = 32
MLA_V = 64
MLA_Q_LORA = 384
MLA_KV_LORA = 256
MLA_SCALE = (MLA_NOPE + MLA_ROPE) ** -0.5
LOG2E = 1.4426950408889634
ROPE_THETA = 10000.0
CM_CH = 512
CM_WIDTH = 31
GDN_HEADS = 4
GDN_DK = 128
GDN_DV = 128
GDN_CONV = 4
GDN_QKV = GDN_HEADS * (2 * GDN_DK + GDN_DV)
XA_HEADS = 4
XA_DIM = D_MODEL // XA_HEADS
MOE_GROUPS = 4
MOE_PER_GROUP = 4
MOE_EXPERTS = MOE_GROUPS * MOE_PER_GROUP
MOE_FF = 256
N_BRANCH = 3
LN_EPS = 1e-5
RMS_EPS = 1e-6
IN_SIZES = (N_BRANCH * D_MODEL, MLA_Q_LORA, MLA_KV_LORA, MLA_ROPE, 2 * CM_CH,
            GDN_QKV, GDN_HEADS * GDN_DV, GDN_HEADS, GDN_HEADS)

LANE = 128
HEAD_SLOT = 128
ROPE_LANE0 = MLA_NOPE
OFF_GATE = 0
OFF_CQ = OFF_GATE + N_BRANCH * D_MODEL
OFF_CKV = OFF_CQ + MLA_Q_LORA
OFF_KR = OFF_CKV + MLA_KV_LORA
OFF_KRR = OFF_KR + LANE
OFF_GLU = OFF_KRR + LANE
OFF_QKV = OFF_GLU + 2 * CM_CH
OFF_GZ = OFF_QKV + GDN_QKV
OFF_GAB = OFF_GZ + GDN_HEADS * GDN_DV
IN_PACKED = OFF_GAB + LANE
CM_HALO = 32
GDN_HALO = 8
VMEM_LIMIT = 56 * 1024 * 1024


def _dot(a, b):
    return jnp.dot(a, b, preferred_element_type=F32)


def _dot_nt(a, b):
    return lax.dot_general(a, b, (((1,), (1,)), ((), ())), preferred_element_type=F32)


def _layer_norm(x, g, b):
    xc = x - jnp.mean(x, axis=-1, keepdims=True)
    var = jnp.mean(xc * xc, axis=-1, keepdims=True)
    return xc * lax.rsqrt(var + LN_EPS) * g + b


def _rms_norm(x, g):
    return x * lax.rsqrt(jnp.mean(x * x, axis=-1, keepdims=True) + RMS_EPS) * g


def _sigmoid(x):
    return 1.0 / (1.0 + jnp.exp(-x))


def _silu(x):
    return x * _sigmoid(x)


def _resident(shape):
    nd = len(shape)
    return pl.BlockSpec(shape, lambda *_: (0,) * nd, pipeline_mode=pl.Buffered(1))


def _params(sem, vmem=VMEM_LIMIT):
    return pltpu.CompilerParams(dimension_semantics=sem, vmem_limit_bytes=vmem)


def _inproj_kernel(x_ref, lng_ref, lnb_ref, w_ref, bg_ref, qn_ref, wq_ref, wqr_ref, kvn_ref,
                   invf_ref, *outs, pre_ln, seq_len, past, tm):
    if pre_ln:
        h_ref, outs = outs[0], outs[1:]
    gates_ref, qp_ref, ckv_ref, kr_ref, glu_ref, qkv_ref, gz_ref, gab_ref = outs
    x = x_ref[...]
    if pre_ln:
        x = _layer_norm(x, lng_ref[...], lnb_ref[...])
        h_ref[...] = x
    xb = x.astype(BF16)

    def grp(off, n):
        return _dot(xb, w_ref[:, off:off + n])

    for j in range(N_BRANCH):
        sl = slice(j * D_MODEL, (j + 1) * D_MODEL)
        gates_ref[:, sl] = _sigmoid(grp(OFF_GATE + j * D_MODEL, D_MODEL) + bg_ref[:, sl])

    row = pl.program_id(0) * tm + lax.broadcasted_iota(jnp.int32, (tm, 1), 0)
    pos = (row % seq_len + past).astype(F32)
    ang_q = pos * invf_ref[0:1, :]
    ang_k = pos * invf_ref[1:2, :]
    cos_q, sin_q = jnp.cos(ang_q) * (MLA_SCALE * LOG2E), jnp.sin(ang_q) * (MLA_SCALE * LOG2E)
    cos_k, sin_k = jnp.cos(ang_k), jnp.sin(ang_k)

    qn = _rms_norm(grp(OFF_CQ, MLA_Q_LORA), qn_ref[...]).astype(BF16)
    for h in range(MLA_HEADS):
        sl = slice(h * HEAD_SLOT, (h + 1) * HEAD_SLOT)
        q1 = _dot(qn, wq_ref[:, sl])
        q2 = _dot(qn, wqr_ref[:, sl])
        qp_ref[:, sl] = (q1 * cos_q + q2 * sin_q).astype(BF16)

    ckv_ref[...] = _rms_norm(grp(OFF_CKV, MLA_KV_LORA), kvn_ref[...])
    kr = grp(OFF_KR, LANE) * cos_k + grp(OFF_KRR, LANE) * sin_k
    kr_ref[...] = kr[:, :MLA_ROPE]

    glu_ref[...] = grp(OFF_GLU, CM_CH) * _sigmoid(grp(OFF_GLU + CM_CH, CM_CH))
    for j in range(GDN_QKV // 512):
        qkv_ref[:, j * 512:(j + 1) * 512] = grp(OFF_QKV + j * 512, 512)
    gz_ref[...] = grp(OFF_GZ, GDN_HEADS * GDN_DV)
    gab_ref[...] = grp(OFF_GAB, LANE)


def _inproj(x, lp, *, pre_ln, seq_len, past, tm=256):
    t = x.shape[0]
    tm = min(tm, t)
    assert t % tm == 0
    row = lambda n: pl.BlockSpec((tm, n), lambda i: (i, 0))
    out_shape = [
        jax.ShapeDtypeStruct((t, N_BRANCH * D_MODEL), F32),
        jax.ShapeDtypeStruct((t, MLA_HEADS * HEAD_SLOT), BF16),
        jax.ShapeDtypeStruct((t, MLA_KV_LORA), F32),
        jax.ShapeDtypeStruct((t, MLA_ROPE), F32),
        jax.ShapeDtypeStruct((t, CM_CH), F32),
        jax.ShapeDtypeStruct((t, GDN_QKV), F32),
        jax.ShapeDtypeStruct((t, GDN_HEADS * GDN_DV), F32),
        jax.ShapeDtypeStruct((t, LANE), F32),
    ]
    out_specs = [row(s.shape[1]) for s in out_shape]
    if pre_ln:
        out_shape = [jax.ShapeDtypeStruct((t, D_MODEL), F32)] + out_shape
        out_specs = [row(D_MODEL)] + out_specs
    return pl.pallas_call(
        functools.partial(_inproj_kernel, pre_ln=pre_ln, seq_len=seq_len, past=past, tm=tm),
        grid=(t // tm,),
        in_specs=[row(D_MODEL), _resident((1, D_MODEL)), _resident((1, D_MODEL)),
                  _resident((D_MODEL, IN_PACKED)), _resident((1, N_BRANCH * D_MODEL)),
                  _resident((1, MLA_Q_LORA)), _resident((MLA_Q_LORA, MLA_HEADS * HEAD_SLOT)),
                  _resident((MLA_Q_LORA, MLA_HEADS * HEAD_SLOT)), _resident((1, MLA_KV_LORA)),
                  _resident((2, LANE))],
        out_specs=out_specs,
        out_shape=out_shape,
        compiler_params=_params(("parallel",)),
        name="inproj",
    )(x, lp["ln_in_g"], lp["ln_in_b"], lp["w_in"], lp["b_gate"], lp["q_norm"], lp["wq"], lp["wqr"],
      lp["kv_norm"], lp["invf"])


def _kvup_kernel(ckv_ref, kr_ref, wk_ref, pk_ref, wv_ref, k_ref, v_ref, *, v_transposed):
    c = ckv_ref[...].astype(BF16)
    r = kr_ref[...].astype(BF16)
    k_ref[...] = (_dot(c, wk_ref[...]) + _dot(r, pk_ref[...])).astype(BF16)
    if v_transposed:
        v_ref[...] = _dot_nt(wv_ref[...], c).astype(BF16)
    else:
        v_ref[...] = _dot(c, wv_ref[...]).astype(BF16)


def _kvup(ckv, kr, lp, v_transposed, tm=512):
    t = ckv.shape[0]
    tm = min(tm, t)
    assert t % tm == 0
    row = lambda n: pl.BlockSpec((tm, n), lambda i: (i, 0))
    dv = MLA_HEADS * MLA_V
    if v_transposed:
        wv, wv_spec = lp["wv_t"], _resident((dv, MLA_KV_LORA))
        v_spec, v_shape = pl.BlockSpec((dv, tm), lambda i: (0, i)), (dv, t)
    else:
        wv, wv_spec = lp["wv"], _resident((MLA_KV_LORA, dv))
        v_spec, v_shape = row(dv), (t, dv)
    return pl.pallas_call(
        functools.partial(_kvup_kernel, v_transposed=v_transposed),
        grid=(t // tm,),
        in_specs=[row(MLA_KV_LORA), row(MLA_ROPE), _resident((MLA_KV_LORA, MLA_HEADS * HEAD_SLOT)),
                  _resident((MLA_ROPE, MLA_HEADS * HEAD_SLOT)), wv_spec],
        out_specs=[row(MLA_HEADS * HEAD_SLOT), v_spec],
        out_shape=[jax.ShapeDtypeStruct((t, MLA_HEADS * HEAD_SLOT), BF16),
                   jax.ShapeDtypeStruct(v_shape, BF16)],
        compiler_params=_params(("parallel",)),
        name="kvup",
    )(ckv, kr, lp["wk"], lp["pk"], wv)


def _attn_prompt_kernel(q_ref, k_ref, vt_ref, o_ref, sa_ref, sb_ref, *, tq):
    i = pl.program_id(2)
    kc = lax.broadcasted_iota(jnp.int32, (tq, tq), 0) // CHUNK
    qc = lax.broadcasted_iota(jnp.int32, (tq, tq), 1) // CHUNK
    visible = kc <= qc
    qs = [q_ref[:, j * HEAD_SLOT:(j + 1) * HEAD_SLOT] for j in range(2)]

    def scores_t(kb):
        start = pl.multiple_of(kb * tq, tq)
        return [_dot_nt(k_ref[pl.ds(start, tq), j * HEAD_SLOT:(j + 1) * HEAD_SLOT], qs[j])
                for j in range(2)]

    def stage_scores(ref, kb):
        for j, s in enumerate(scores_t(kb)):
            ref[j] = s

    def update(kb, s_pair, c, masked=False):
        start = pl.multiple_of(kb * tq, tq)
        out = []
        for j in range(2):
            m, l, acc = c[3 * j:3 * j + 3]
            s = jnp.where(visible, s_pair[j], -jnp.inf) if masked else s_pair[j]
            m_new = jnp.maximum(m, jnp.max(s, axis=0, keepdims=True))
            alpha = jnp.exp2(m - m_new)
            p = jnp.exp2(s - m_new)
            vt = vt_ref[j * MLA_V:(j + 1) * MLA_V, pl.ds(start, tq)]
            out += [m_new, alpha * l + jnp.sum(p, axis=0, keepdims=True),
                    alpha * acc + _dot(vt, p.astype(BF16))]
        return tuple(out)

    odd = i % 2
    stage_scores(sa_ref, jnp.minimum(odd, i))
    c = []
    for j in range(2):
        c += [jnp.full((1, tq), -jnp.inf, F32), jnp.zeros((1, tq), F32), jnp.zeros((MLA_V, tq), F32)]
    c = update(i, scores_t(i), tuple(c), masked=True)
    c = lax.fori_loop(0, odd, lambda _, cc: update(0, scores_t(0), cc), c)

    def pair(t, cc):
        b0 = odd + 2 * t
        stage_scores(sb_ref, b0 + 1)
        cc = update(b0, (sa_ref[0], sa_ref[1]), cc)
        stage_scores(sa_ref, jnp.minimum(b0 + 2, i))
        return update(b0 + 1, (sb_ref[0], sb_ref[1]), cc)

    c = lax.fori_loop(0, (i - odd) // 2, pair, c)
    o_t = jnp.concatenate([c[2] / c[1], c[5] / c[4]], axis=0)
    o_ref[...] = o_t.T.astype(BF16)


def _attn_prompt(qp, kp, vt, batch, seq_len, tq=256):
    assert seq_len % tq == 0 and tq % CHUNK == 0
    nq = seq_len // tq
    t = batch * seq_len
    return pl.pallas_call(
        functools.partial(_attn_prompt_kernel, tq=tq),
        grid=(batch, MLA_HEADS // 2, nq),
        in_specs=[pl.BlockSpec((tq, 2 * HEAD_SLOT), lambda b, hp, i: (b * nq + i, hp)),
                  pl.BlockSpec((seq_len, 2 * HEAD_SLOT), lambda b, hp, i: (b, hp)),
                  pl.BlockSpec((2 * MLA_V, seq_len), lambda b, hp, i: (hp, b))],
        out_specs=pl.BlockSpec((tq, 2 * MLA_V), lambda b, hp, i: (b * nq + i, hp)),
        out_shape=jax.ShapeDtypeStruct((t, MLA_HEADS * MLA_V), BF16),
        scratch_shapes=[pltpu.VMEM((2, tq, tq), F32), pltpu.VMEM((2, tq, tq), F32)],
        compiler_params=_params(("parallel", "parallel", "arbitrary")),
        name="attn_prompt",
    )(qp, kp, vt)


def _attn_sample_kernel(q_ref, kp_ref, kn_ref, vp_ref, vn_ref, o_ref):
    vp, vn = vp_ref[...], vn_ref[...]
    outs = []
    for j in range(2):
        sl = slice(j * HEAD_SLOT, (j + 1) * HEAD_SLOT)
        q = q_ref[:, sl]
        sp = _dot_nt(q, kp_ref[:, sl])
        sn = _dot_nt(q, kn_ref[:, sl])
        m = jnp.maximum(jnp.max(sp, axis=-1, keepdims=True), jnp.max(sn, axis=-1, keepdims=True))
        pp, pn = jnp.exp2(sp - m), jnp.exp2(sn - m)
        l = jnp.sum(pp, axis=-1, keepdims=True) + jnp.sum(pn, axis=-1, keepdims=True)
        outs.append((_dot(pp.astype(BF16), vp) + _dot(pn.astype(BF16), vn)) / l)
    lane = lax.broadcasted_iota(jnp.int32, outs[0].shape, 1)
    o_ref[...] = jnp.where(lane < MLA_V, outs[0], outs[1]).astype(BF16)


def _attn_sample(qp, kpast, knew, vpast, vnew, batch, seq_len, past):
    blk = lambda rows, n: pl.BlockSpec((rows, n), lambda b, hp: (b, hp))
    return pl.pallas_call(
        _attn_sample_kernel,
        grid=(batch, MLA_HEADS // 2),
        in_specs=[blk(seq_len, 2 * HEAD_SLOT), blk(past, 2 * HEAD_SLOT), blk(seq_len, 2 * HEAD_SLOT),
                  blk(past, 2 * MLA_V), blk(seq_len, 2 * MLA_V)],
        out_specs=blk(seq_len, 2 * MLA_V),
        out_shape=jax.ShapeDtypeStruct((batch * seq_len, MLA_HEADS * MLA_V), BF16),
        compiler_params=_params(("parallel", "parallel")),
        name="attn_sample",
    )(qp, kpast, knew, vpast, vnew)


def _conformer_kernel(x_ref, st_ref, w_ref, b_ref, g_ref, beta_ref, o_ref, xbuf, *, tl, sub):
    @pl.when(pl.program_id(1) == 0)
    def _():
        xbuf[0:CM_HALO, :] = st_ref[0]

    xbuf[CM_HALO:CM_HALO + tl, :] = x_ref[...]
    first = CM_HALO - (CM_WIDTH - 1)
    for r in range(tl // sub):
        acc = jnp.zeros((sub, CM_CH), F32) + b_ref[...]
        for j in range(CM_WIDTH):
            acc = acc + w_ref[j:j + 1, :] * xbuf[r * sub + first + j:r * sub + first + j + sub, :]
        y = _layer_norm(acc, g_ref[...], beta_ref[...])
        o_ref[r * sub:(r + 1) * sub, :] = _silu(y).astype(BF16)
    xbuf[0:CM_HALO, :] = xbuf[tl:tl + CM_HALO, :]


def _conformer(glu, state, lp, batch, seq_len):
    tl = min(seq_len, 256)
    assert seq_len % tl == 0 and tl >= CM_HALO
    nl = seq_len // tl
    return pl.pallas_call(
        functools.partial(_conformer_kernel, tl=tl, sub=32),
        grid=(batch, nl),
        in_specs=[pl.BlockSpec((tl, CM_CH), lambda b, l: (b * nl + l, 0)),
                  pl.BlockSpec((1, CM_HALO, CM_CH), lambda b, l: (b, 0, 0)),
                  _resident((CM_WIDTH, CM_CH)), _resident((1, CM_CH)), _resident((1, CM_CH)),
                  _resident((1, CM_CH))],
        out_specs=pl.BlockSpec((tl, CM_CH), lambda b, l: (b * nl + l, 0)),
        out_shape=jax.ShapeDtypeStruct((batch * seq_len, CM_CH), BF16),
        scratch_shapes=[pltpu.VMEM((CM_HALO + tl, CM_CH), F32)],
        compiler_params=_params(("parallel", "arbitrary")),
        name="conformer",
    )(glu, state, lp["cm_w_dw"], lp["cm_b_dw"], lp["cm_ln_g"], lp["cm_ln_b"])


def _split_bf16(a):
    hi = a.astype(BF16)
    return hi, (a - hi.astype(F32)).astype(BF16)


def _dot3(a, b):
    (ah, al), (bh, bl) = a, b
    return _dot(ah, bh) + (_dot(ah, bl) + _dot(al, bh))


def _gdn_kernel(qkv_ref, gab_ref, gz_ref, cst_ref, s0_ref, wc_ref, alog_ref, dtb_ref, gn_ref,
                o_ref, s_ref, xbuf, xc, *, rows):
    @pl.when(pl.program_id(1) == 0)
    def _():
        xbuf[0:GDN_HALO, :] = cst_ref[0]
        s_ref[...] = s0_ref[...]

    xbuf[GDN_HALO:GDN_HALO + rows, :] = qkv_ref[...]
    for c in range(GDN_QKV // LANE):
        sl = slice(c * LANE, (c + 1) * LANE)
        acc = wc_ref[GDN_CONV - 1:GDN_CONV, sl] * xbuf[GDN_HALO:GDN_HALO + rows, sl]
        for j in range(GDN_CONV - 1):
            lo = GDN_HALO - (GDN_CONV - 1) + j
            acc = acc + wc_ref[j:j + 1, sl] * xbuf[lo:lo + rows, sl]
        xc[:, sl] = _silu(acc)
    xbuf[0:GDN_HALO, :] = xbuf[rows:rows + GDN_HALO, :]

    gab = gab_ref[...]
    z = gab + dtb_ref[...]
    softplus = jnp.maximum(z, 0.0) + jnp.log(1.0 + jnp.exp(-jnp.abs(z)))
    g = -jnp.exp(alog_ref[...]) * softplus
    beta_all = _sigmoid(gab)
    ri = lax.broadcasted_iota(jnp.int32, (rows, rows), 0)
    ci = lax.broadcasted_iota(jnp.int32, (rows, rows), 1)
    tri = jnp.where((ri >= ci) & (ri // CHUNK == ci // CHUNK), 1.0, 0.0).astype(BF16)
    g1 = g.astype(BF16)
    r1 = g - g1.astype(F32)
    g2 = r1.astype(BF16)
    g3 = (r1 - g2.astype(F32)).astype(BF16)
    gcs = _dot(tri, g1) + _dot(tri, g2) + _dot(tri, g3)
    gcs_t = gcs.T

    i64 = lax.broadcasted_iota(jnp.int32, (CHUNK, CHUNK), 0)
    j64 = lax.broadcasted_iota(jnp.int32, (CHUNK, CHUNK), 1)
    causal = i64 >= j64
    strict = i64 > j64
    eye = jnp.where(i64 == j64, 1.0, 0.0).astype(F32)
    dq, dk = GDN_HEADS * GDN_DK, GDN_HEADS * GDN_DK
    n_chunks = rows // CHUNK
    units = [(c, h) for c in range(n_chunks) for h in range(GDN_HEADS)]

    U = {}
    for c, h in units:
        rs = slice(c * CHUNK, (c + 1) * CHUNK)
        q = xc[rs, h * GDN_DK:(h + 1) * GDN_DK]
        k = xc[rs, dq + h * GDN_DK:dq + (h + 1) * GDN_DK]
        v = xc[rs, dq + dk + h * GDN_DV:dq + dk + (h + 1) * GDN_DV]
        q = q * lax.rsqrt(jnp.sum(q * q, axis=-1, keepdims=True) + 1e-6) * (GDN_DK ** -0.5)
        k = k * lax.rsqrt(jnp.sum(k * k, axis=-1, keepdims=True) + 1e-6)
        gcol = gcs[rs, h:h + 1]
        grow = gcs_t[h:h + 1, rs]
        beta = beta_all[rs, GDN_HEADS + h:GDN_HEADS + h + 1]
        decay = jnp.exp(jnp.where(causal, gcol - grow, -jnp.inf))
        kbeta = k * beta
        egc = jnp.exp(gcol)
        glast = gcol[CHUNK - 1:CHUNK, :]
        U[c, h] = dict(
            a=jnp.where(strict, _dot_nt(kbeta, k) * decay, 0.0),
            qk=(_dot_nt(q, k) * decay).astype(BF16),
            rhs=jnp.concatenate([v * beta, kbeta * egc], axis=1),
            qe=(q * egc).astype(BF16),
            kd=(k * jnp.exp(glast - gcol)).astype(BF16),
            sdecay=jnp.exp(glast))

    x = {u: eye - U[u]["a"] for u in units}
    p = {u: _split_bf16(U[u]["a"]) for u in units}
    k2 = 2
    while k2 < CHUNK:
        p2 = {u: _dot3(p[u], p[u]) for u in units}
        p = {u: _split_bf16(p2[u]) for u in units}
        x = {u: x[u] + _dot3(_split_bf16(x[u]), p[u]) for u in units}
        k2 *= 2
    sol = {u: _dot3(_split_bf16(x[u]), _split_bf16(U[u]["rhs"])) for u in units}

    for c in range(n_chunks):
        rs = slice(c * CHUNK, (c + 1) * CHUNK)
        s_old = [s_ref[0, h] for h in range(GDN_HEADS)]
        s_bf = [s.astype(BF16) for s in s_old]
        ws = [_dot(sol[c, h][:, GDN_DV:].astype(BF16), s_bf[h]) for h in range(GDN_HEADS)]
        qs = [_dot(U[c, h]["qe"], s_bf[h]) for h in range(GDN_HEADS)]
        v_new = [(sol[c, h][:, :GDN_DV] - ws[h]).astype(BF16) for h in range(GDN_HEADS)]
        for h in range(GDN_HEADS):
            s_ref[0, h] = s_old[h] * U[c, h]["sdecay"] + lax.dot_general(
                U[c, h]["kd"], v_new[h], (((0,), (0,)), ((), ())), preferred_element_type=F32)
        for h in range(GDN_HEADS):
            o = qs[h] + _dot(U[c, h]["qk"], v_new[h])
            gate = _silu(gz_ref[rs, h * GDN_DV:(h + 1) * GDN_DV])
            o_ref[rs, h * GDN_DV:(h + 1) * GDN_DV] = (_rms_norm(o, gn_ref[...]) * gate).astype(BF16)


def _gdn(qkv, gab, gz, conv_state, s0, lp, batch, seq_len):
    rows = min(seq_len, 256)
    assert seq_len % rows == 0 and rows % CHUNK == 0
    nl = seq_len // rows
    tile = lambda n: pl.BlockSpec((rows, n), lambda b, l: (b * nl + l, 0))
    return pl.pallas_call(
        functools.partial(_gdn_kernel, rows=rows),
        grid=(batch, nl),
        in_specs=[tile(GDN_QKV), tile(LANE), tile(GDN_HEADS * GDN_DV),
                  pl.BlockSpec((1, GDN_HALO, GDN_QKV), lambda b, l: (b, 0, 0)),
                  pl.BlockSpec((1, GDN_HEADS, GDN_DK, GDN_DV), lambda b, l: (b, 0, 0, 0)),
                  _resident((GDN_CONV, GDN_QKV)), _resident((1, LANE)), _resident((1, LANE)),
                  _resident((1, GDN_DV))],
        out_specs=[tile(GDN_HEADS * GDN_DV),
                   pl.BlockSpec((1, GDN_HEADS, GDN_DK, GDN_DV), lambda b, l: (b, 0, 0, 0))],
        out_shape=[jax.ShapeDtypeStruct((batch * seq_len, GDN_HEADS * GDN_DV), BF16),
                   jax.ShapeDtypeStruct((batch, GDN_HEADS, GDN_DK, GDN_DV), F32)],
        scratch_shapes=[pltpu.VMEM((GDN_HALO + rows, GDN_QKV), F32),
                        pltpu.VMEM((rows, GDN_QKV), F32)],
        compiler_params=_params(("parallel", "arbitrary")),
        name="gdn",
    )(qkv, gab, gz, conv_state, s0, lp["g_w_conv"], lp["g_a_log"], lp["g_dt_bias"], lp["g_norm"])


def _route(lg):
    lane = lax.broadcasted_iota(jnp.int32, lg.shape, 1)
    is_g = lane < MOE_GROUPS
    gl = jnp.where(is_g, lg, -jnp.inf)
    gmax = jnp.max(gl, axis=-1, keepdims=True)
    gsel = jnp.min(jnp.where(gl == gmax, lane, LANE), axis=-1, keepdims=True)
    gw = 1.0 / jnp.sum(jnp.where(is_g, jnp.exp(lg - gmax), 0.0), axis=-1, keepdims=True)
    e_lo = MOE_GROUPS + gsel * MOE_PER_GROUP
    in_grp = (lane >= e_lo) & (lane < e_lo + MOE_PER_GROUP)
    el = jnp.where(in_grp, lg, -jnp.inf)
    v1 = jnp.max(el, axis=-1, keepdims=True)
    i1 = jnp.min(jnp.where(el == v1, lane, LANE), axis=-1, keepdims=True)
    rest = in_grp & (lane != i1)
    el2 = jnp.where(rest, lg, -jnp.inf)
    v2 = jnp.max(el2, axis=-1, keepdims=True)
    i2 = jnp.min(jnp.where(rest & (el2 == v2), lane, LANE), axis=-1, keepdims=True)
    e2 = jnp.exp(v2 - v1)
    den = 1.0 + e2
    return jnp.where(lane == i1, gw / den, 0.0) + jnp.where(lane == i2, gw * e2 / den, 0.0)


def _merge_kernel(h_ref, g_ref, oa_ref, cb_ref, oc_ref, wpa_ref, wpb_ref, wpc_ref, wout_ref,
                  l1g_ref, l1b_ref, wq_ref, mk_ref, mv_ref, wo_ref, l2g_ref, l2b_ref, wr_ref, br_ref,
                  h2_ref, comb_ref, *, alpha):
    h = h_ref[...]
    merged = (g_ref[:, 0:D_MODEL] * _dot(oa_ref[...], wpa_ref[...])
              + g_ref[:, D_MODEL:2 * D_MODEL] * _dot(cb_ref[...], wpb_ref[...])
              + g_ref[:, 2 * D_MODEL:3 * D_MODEL] * _dot(oc_ref[...], wpc_ref[...]))
    t = _dot(merged.astype(BF16), wout_ref[...])
    h1 = _layer_norm(alpha * h + t, l1g_ref[...], l1b_ref[...])
    q = _dot(h1.astype(BF16), wq_ref[...]).astype(BF16)
    xo = jnp.zeros_like(h)
    for hh in range(XA_HEADS):
        sl = slice(hh * XA_DIM, (hh + 1) * XA_DIM)
        s = _dot_nt(q[:, sl], mk_ref[0, :, sl]) * (XA_DIM ** -0.5)
        p = jnp.exp(s - jnp.max(s, axis=-1, keepdims=True))
        p = p / jnp.sum(p, axis=-1, keepdims=True)
        o = _dot(p.astype(BF16), mv_ref[0, :, sl])
        xo = xo + _dot(o.astype(BF16), wo_ref[sl, :])
    h2 = _layer_norm(alpha * h1 + xo, l2g_ref[...], l2b_ref[...])
    h2_ref[...] = h2
    comb_ref[...] = _route(_dot(h2.astype(BF16), wr_ref[...]) + br_ref[...])


def _merge(h, gates, oa, cb, oc, mem_k, mem_v, lp, batch, seq_len, alpha):
    tm = min(seq_len, 256)
    assert seq_len % tm == 0
    per_b = seq_len // tm
    t = batch * seq_len
    mem_len = mem_k.shape[1]
    row = lambda n: pl.BlockSpec((tm, n), lambda i: (i, 0))
    mem = pl.BlockSpec((1, mem_len, D_MODEL), lambda i: (i // per_b, 0, 0))
    half = MLA_HEADS * MLA_V
    return pl.pallas_call(
        functools.partial(_merge_kernel, alpha=alpha),
        grid=(t // tm,),
        in_specs=[row(D_MODEL), row(N_BRANCH * D_MODEL), row(half), row(CM_CH), row(GDN_HEADS * GDN_DV),
                  _resident((half, D_MODEL)), _resident((CM_CH, D_MODEL)),
                  _resident((GDN_HEADS * GDN_DV, D_MODEL)), _resident((D_MODEL, D_MODEL)),
                  _resident((1, D_MODEL)), _resident((1, D_MODEL)), _resident((D_MODEL, D_MODEL)),
                  mem, mem, _resident((D_MODEL, D_MODEL)), _resident((1, D_MODEL)),
                  _resident((1, D_MODEL)), _resident((D_MODEL, LANE)), _resident((1, LANE))],
        out_specs=[row(D_MODEL), row(LANE)],
        out_shape=[jax.ShapeDtypeStruct((t, D_MODEL), F32), jax.ShapeDtypeStruct((t, LANE), F32)],
        compiler_params=_params(("parallel",)),
        name="merge",
    )(h, gates, oa, cb, oc, lp["w_pa"], lp["w_pb"], lp["w_pc"], lp["w_out"], lp["ln1_g"], lp["ln1_b"],
      lp["xa_wq"], mem_k, mem_v, lp["xa_wo"], lp["ln2_g"], lp["ln2_b"], lp["w_route"], lp["b_route"])


def _moe_kernel(x_ref, comb_ref, wg_ref, wu_ref, wd_ref, lg_ref, lb_ref, o_ref, xb_ref, acc_ref, *, alpha):
    e = pl.program_id(1)

    @pl.when(e == 0)
    def _():
        xb_ref[...] = x_ref[...].astype(BF16)
        acc_ref[...] = jnp.zeros_like(acc_ref)

    xb = xb_ref[...]
    lane = lax.broadcasted_iota(jnp.int32, comb_ref.shape, 1)
    cw = jnp.sum(jnp.where(lane == e + MOE_GROUPS, comb_ref[...], 0.0), axis=-1, keepdims=True)
    a = _dot(xb, wg_ref[0])
    b = _dot(xb, wu_ref[0])
    hid = _silu(a) * b * cw
    acc_ref[...] += _dot(hid.astype(BF16), wd_ref[0])

    @pl.when(e == MOE_EXPERTS - 1)
    def _():
        o_ref[...] = _layer_norm(alpha * x_ref[...] + acc_ref[...], lg_ref[...], lb_ref[...])


def _moe(x, comb, lp, alpha):
    t = x.shape[0]
    tm = min(t, 1024)
    assert t % tm == 0
    row = lambda n: pl.BlockSpec((tm, n), lambda i, e: (i, 0))
    return pl.pallas_call(
        functools.partial(_moe_kernel, alpha=alpha),
        grid=(t // tm, MOE_EXPERTS),
        in_specs=[row(D_MODEL), row(LANE),
                  pl.BlockSpec((1, D_MODEL, MOE_FF), lambda i, e: (e, 0, 0)),
                  pl.BlockSpec((1, D_MODEL, MOE_FF), lambda i, e: (e, 0, 0)),
                  pl.BlockSpec((1, MOE_FF, D_MODEL), lambda i, e: (e, 0, 0)),
                  pl.BlockSpec((1, D_MODEL), lambda i, e: (0, 0)),
                  pl.BlockSpec((1, D_MODEL), lambda i, e: (0, 0))],
        out_specs=row(D_MODEL),
        out_shape=jax.ShapeDtypeStruct((t, D_MODEL), F32),
        scratch_shapes=[pltpu.VMEM((tm, D_MODEL), BF16), pltpu.VMEM((tm, D_MODEL), F32)],
        compiler_params=_params(("parallel", "arbitrary")),
        name="moe",
    )(x, comb, lp["moe_wg"], lp["moe_wu"], lp["moe_wd"], lp["ln3_g"], lp["ln3_b"])


def _memkv_kernel(x_ref, wk_ref, wv_ref, k_ref, v_ref):
    xb = x_ref[...].astype(BF16)
    k_ref[...] = _dot(xb, wk_ref[...])
    v_ref[...] = _dot(xb, wv_ref[...])


def _memkv(mem, lp):
    t = mem.shape[0]
    full = lambda r, c: pl.BlockSpec((r, c), lambda i: (0, 0))
    return pl.pallas_call(
        _memkv_kernel,
        grid=(1,),
        in_specs=[full(t, D_MODEL), full(D_MODEL, D_MODEL), full(D_MODEL, D_MODEL)],
        out_specs=[full(t, D_MODEL), full(t, D_MODEL)],
        out_shape=[jax.ShapeDtypeStruct((t, D_MODEL), F32)] * 2,
        compiler_params=_params(("arbitrary",)),
        name="memkv",
    )(mem, lp["xa_wk"], lp["xa_wv"])


def _rot_half_cols(w):
    half = w.shape[-1] // 2
    return jnp.concatenate([-w[..., half:], w[..., :half]], axis=-1)


def _pack_layer(l, p):
    d = D_MODEL
    cols, start = [], 0
    for s in IN_SIZES:
        cols.append(p["w_in"][l][:, start:start + s])
        start += s
    w_gate, w_cq, w_ckv, w_kr, w_glu, w_qkv, w_gz, w_ga, w_gb = cols
    zpad = lambda n: jnp.zeros((d, n), F32)
    w_in = jnp.concatenate([
        w_gate, w_cq, w_ckv,
        w_kr, zpad(LANE - MLA_ROPE),
        _rot_half_cols(w_kr), zpad(LANE - MLA_ROPE),
        w_glu, w_qkv, w_gz,
        w_ga, w_gb, zpad(LANE - 2 * GDN_HEADS)], axis=1).astype(BF16)
    assert w_in.shape[1] == IN_PACKED

    wuq = p["mla_w_uq"][l].reshape(MLA_Q_LORA, MLA_HEADS, MLA_NOPE + MLA_ROPE)
    nope, rope = wuq[..., :MLA_NOPE], wuq[..., MLA_NOPE:]
    pad_q = HEAD_SLOT - MLA_NOPE - MLA_ROPE
    zq = lambda n: jnp.zeros((MLA_Q_LORA, MLA_HEADS, n), F32)
    wq = jnp.concatenate([nope, rope, zq(pad_q)], -1).reshape(MLA_Q_LORA, MLA_HEADS * HEAD_SLOT)
    wqr = jnp.concatenate([zq(MLA_NOPE), _rot_half_cols(rope), zq(pad_q)], -1).reshape(
        MLA_Q_LORA, MLA_HEADS * HEAD_SLOT)

    wukv = p["mla_w_ukv"][l].reshape(MLA_KV_LORA, MLA_HEADS, MLA_NOPE + MLA_V)
    wk = jnp.concatenate([wukv[..., :MLA_NOPE],
                          jnp.zeros((MLA_KV_LORA, MLA_HEADS, HEAD_SLOT - MLA_NOPE), F32)], -1)
    wk = wk.reshape(MLA_KV_LORA, MLA_HEADS * HEAD_SLOT)
    wv = wukv[..., MLA_NOPE:].reshape(MLA_KV_LORA, MLA_HEADS * MLA_V)
    place = jnp.concatenate([jnp.zeros((MLA_ROPE, ROPE_LANE0), F32), jnp.eye(MLA_ROPE, dtype=F32),
                             jnp.zeros((MLA_ROPE, pad_q), F32)], -1)
    pk = jnp.tile(place, (1, MLA_HEADS))

    half = MLA_ROPE // 2
    inv = ROPE_THETA ** (-jnp.arange(half, dtype=F32) / half)
    inv2 = jnp.concatenate([inv, inv])
    invf = jnp.stack([
        jnp.concatenate([jnp.zeros((ROPE_LANE0,), F32), inv2, jnp.zeros((pad_q,), F32)]),
        jnp.concatenate([inv2, jnp.zeros((LANE - MLA_ROPE,), F32)])])

    lane_pad = lambda v: jnp.concatenate([v, jnp.zeros((LANE - v.shape[0],), F32)])[None, :]
    w_route = jnp.concatenate([p["moe_w_rg"][l], p["moe_w_re"][l],
                               zpad(LANE - MOE_GROUPS - MOE_EXPERTS)], axis=1).astype(BF16)
    b_route = lane_pad(jnp.concatenate([p["moe_b_rg"][l], p["moe_b_re"][l]]))
    r1 = lambda v: v.reshape(1, -1)
    return dict(
        ln_in_g=r1(p["ln_in_g"]), ln_in_b=r1(p["ln_in_b"]),
        w_in=w_in, b_gate=r1(p["b_gate"][l]), q_norm=r1(p["mla_q_norm"][l]),
        wq=wq.astype(BF16), wqr=wqr.astype(BF16), kv_norm=r1(p["mla_kv_norm"][l]), invf=invf,
        wk=wk.astype(BF16), pk=pk.astype(BF16), wv=wv.astype(BF16), wv_t=wv.T.astype(BF16),
        cm_w_dw=p["cm_w_dw"][l], cm_b_dw=r1(p["cm_b_dw"][l]), cm_ln_g=r1(p["cm_ln_g"][l]),
        cm_ln_b=r1(p["cm_ln_b"][l]),
        g_w_conv=p["gdn_w_conv"][l], g_a_log=lane_pad(p["gdn_a_log"][l]),
        g_dt_bias=lane_pad(p["gdn_dt_bias"][l]), g_norm=r1(p["gdn_norm"][l]),
        w_pa=p["w_proj_a"][l].astype(BF16), w_pb=p["w_proj_b"][l].astype(BF16),
        w_pc=p["w_proj_c"][l].astype(BF16), w_out=p["w_out"][l].astype(BF16),
        ln1_g=r1(p["ln1_g"][l]), ln1_b=r1(p["ln1_b"][l]),
        xa_wq=p["xa_w_q"][l].astype(BF16), xa_wk=p["xa_w_k"][l].astype(BF16),
        xa_wv=p["xa_w_v"][l].astype(BF16), xa_wo=p["xa_w_o"][l].astype(BF16),
        ln2_g=r1(p["ln2_g"][l]), ln2_b=r1(p["ln2_b"][l]),
        w_route=w_route, b_route=b_route,
        moe_wg=p["moe_w_gate"][l].astype(BF16), moe_wu=p["moe_w_up"][l].astype(BF16),
        moe_wd=p["moe_w_down"][l].astype(BF16),
        ln3_g=r1(p["ln3_g"][l]), ln3_b=r1(p["ln3_b"][l]),
    )


def _front_pad_rows(state, rows):
    b, r, c = state.shape
    return jnp.concatenate([jnp.zeros((b, rows - r, c), state.dtype), state], axis=1)


def _layer(x, lp, *, first, batch, seq_len, past, alpha, cache, cm_state, gc_state, s0, mem_k, mem_v):
    res = _inproj(x, lp, pre_ln=first, seq_len=seq_len, past=past)
    if first:
        h, res = res[0], res[1:]
    else:
        h = x
    gates, qp, ckv, kr, glu, qkv, gz, gab = res
    kn, vn = _kvup(ckv, kr, lp, v_transposed=cache is None)
    if cache is None:
        oa = _attn_prompt(qp, kn, vn, batch, seq_len)
    else:
        ckv_past, kr_past = cache
        kpast, vpast = _kvup(ckv_past.reshape(batch * past, MLA_KV_LORA),
                             kr_past.reshape(batch * past, MLA_ROPE), lp, v_transposed=False)
        oa = _attn_sample(qp, kpast, kn, vpast, vn, batch, seq_len, past)
    cb = _conformer(glu, _front_pad_rows(cm_state, CM_HALO), lp, batch, seq_len)
    oc, s_new = _gdn(qkv, gab, gz, _front_pad_rows(gc_state, GDN_HALO), s0, lp, batch, seq_len)
    h2, comb = _merge(h, gates, oa, cb, oc, mem_k.astype(BF16), mem_v.astype(BF16), lp, batch, seq_len, alpha)
    h3 = _moe(h2, comb, lp, alpha)
    assert seq_len >= CM_WIDTH - 1
    states = (ckv.reshape(batch, seq_len, MLA_KV_LORA), kr.reshape(batch, seq_len, MLA_ROPE),
              glu.reshape(batch, seq_len, CM_CH)[:, seq_len - (CM_WIDTH - 1):],
              qkv.reshape(batch, seq_len, GDN_QKV)[:, seq_len - (GDN_CONV - 1):], s_new)
    return h3, states


def kernel(x_prompt, x_sample, mem_prompt, cache_mla_ckv, cache_mla_krope, state_cm_conv, state_gdn_conv, state_gdn, cache_mem_k, cache_mem_v, ln_in_g, ln_in_b, w_in, b_gate, mla_q_norm, mla_w_uq, mla_kv_norm, mla_w_ukv, w_proj_a, cm_w_dw, cm_b_dw, cm_ln_g, cm_ln_b, w_proj_b, gdn_w_conv, gdn_a_log, gdn_dt_bias, gdn_norm, w_proj_c, w_out, ln1_g, ln1_b, xa_w_q, xa_w_k, xa_w_v, xa_w_o, ln2_g, ln2_b, moe_w_rg, moe_b_rg, moe_w_re, moe_b_re, moe_w_gate, moe_w_up, moe_w_down, ln3_g, ln3_b):
    p = dict(ln_in_g=ln_in_g, ln_in_b=ln_in_b, w_in=w_in, b_gate=b_gate, mla_q_norm=mla_q_norm,
             mla_w_uq=mla_w_uq, mla_kv_norm=mla_kv_norm, mla_w_ukv=mla_w_ukv, w_proj_a=w_proj_a,
             cm_w_dw=cm_w_dw, cm_b_dw=cm_b_dw, cm_ln_g=cm_ln_g, cm_ln_b=cm_ln_b, w_proj_b=w_proj_b,
             gdn_w_conv=gdn_w_conv, gdn_a_log=gdn_a_log, gdn_dt_bias=gdn_dt_bias, gdn_norm=gdn_norm,
             w_proj_c=w_proj_c, w_out=w_out, ln1_g=ln1_g, ln1_b=ln1_b, xa_w_q=xa_w_q, xa_w_k=xa_w_k,
             xa_w_v=xa_w_v, xa_w_o=xa_w_o, ln2_g=ln2_g, ln2_b=ln2_b, moe_w_rg=moe_w_rg,
             moe_b_rg=moe_b_rg, moe_w_re=moe_w_re, moe_b_re=moe_b_re, moe_w_gate=moe_w_gate,
             moe_w_up=moe_w_up, moe_w_down=moe_w_down, ln3_g=ln3_g, ln3_b=ln3_b)
    depth = w_in.shape[0]
    alpha = (2 * depth) ** 0.25
    b_p, l_p, d = x_prompt.shape
    b_s, l_s, _ = x_sample.shape
    past = cache_mla_ckv.shape[2]
    mem_len = mem_prompt.shape[1]
    hp = x_prompt.reshape(b_p * l_p, d)
    hs = x_sample.reshape(b_s * l_s, d)
    cm0 = jnp.zeros((b_p, CM_WIDTH - 1, CM_CH), F32)
    gc0 = jnp.zeros((b_p, GDN_CONV - 1, GDN_QKV), F32)
    s0 = jnp.zeros((b_p, GDN_HEADS, GDN_DK, GDN_DV), F32)
    outs_p = [[] for _ in range(7)]
    outs_s = [[] for _ in range(5)]
    for l in range(depth):
        lp = _pack_layer(l, p)
        mk, mv = _memkv(mem_prompt.reshape(b_p * mem_len, d), lp)
        mk = mk.reshape(b_p, mem_len, d)
        mv = mv.reshape(b_p, mem_len, d)
        hp, st_p = _layer(hp, lp, first=(l == 0), batch=b_p, seq_len=l_p, past=0, alpha=alpha,
                          cache=None, cm_state=cm0, gc_state=gc0, s0=s0, mem_k=mk, mem_v=mv)
        hs, st_s = _layer(hs, lp, first=(l == 0), batch=b_s, seq_len=l_s, past=past, alpha=alpha,
                          cache=(cache_mla_ckv[l], cache_mla_krope[l]), cm_state=state_cm_conv[l],
                          gc_state=state_gdn_conv[l], s0=state_gdn[l],
                          mem_k=cache_mem_k[l].reshape(b_s, mem_len, d),
                          mem_v=cache_mem_v[l].reshape(b_s, mem_len, d))
        mk4 = mk.reshape(b_p, mem_len, XA_HEADS, XA_DIM)
        mv4 = mv.reshape(b_p, mem_len, XA_HEADS, XA_DIM)
        for lst, arr in zip(outs_p, st_p + (mk4, mv4)):
            lst.append(arr)
        for lst, arr in zip(outs_s, st_s):
            lst.append(arr)
    ckv_p, kr_p, cm_p, gc_p, gdn_p, mk_p, mv_p = [jnp.stack(a) for a in outs_p]
    ckv_s, kr_s, cm_s, gc_s, gdn_s = [jnp.stack(a) for a in outs_s]
    return (hp.reshape(b_p, l_p, d), hs.reshape(b_s, l_s, d), ckv_p, kr_p, ckv_s, kr_s,
            cm_p, cm_s, gc_p, gc_s, gdn_p, gdn_s, mk_p, mv_p)
```

```python
import functools

import jax
import jax.numpy as jnp
from jax import lax
from jax.experimental import pallas as pl
from jax.experimental.pallas import tpu as pltpu

F32 = jnp.float32
BF16 = jnp.bfloat16

D_MODEL = 1024
CHUNK = 64
MLA_HEADS = 8
MLA_NOPE = 64
MLA_ROPE = 32
MLA_V = 64
MLA_Q_LORA = 384
MLA_KV_LORA = 256
MLA_SCALE = (MLA_NOPE + MLA_ROPE) ** -0.5
LOG2E = 1.4426950408889634
ROPE_THETA = 10000.0
CM_CH = 512
CM_WIDTH = 31
GDN_HEADS = 4
GDN_DK = 128
GDN_DV = 128
GDN_CONV = 4
GDN_QKV = GDN_HEADS * (2 * GDN_DK + GDN_DV)
XA_HEADS = 4
XA_DIM = D_MODEL // XA_HEADS
MOE_GROUPS = 4
MOE_PER_GROUP = 4
MOE_EXPERTS = MOE_GROUPS * MOE_PER_GROUP
MOE_FF = 256
N_BRANCH = 3
LN_EPS = 1e-5
RMS_EPS = 1e-6
IN_SIZES = (N_BRANCH * D_MODEL, MLA_Q_LORA, MLA_KV_LORA, MLA_ROPE, 2 * CM_CH,
            GDN_QKV, GDN_HEADS * GDN_DV, GDN_HEADS, GDN_HEADS)

LANE = 128
BF16_ROWS = 16
VT_ROWS = MLA_V + BF16_ROWS
SUBLANE = 8
HEAD_SLOT = 128
ROPE_LANE0 = MLA_NOPE
OFF_GATE = 0
OFF_CQ = OFF_GATE + N_BRANCH * D_MODEL
OFF_CKV = OFF_CQ + MLA_Q_LORA
OFF_KR = OFF_CKV + MLA_KV_LORA
OFF_KRR = OFF_KR + LANE
OFF_GLU = OFF_KRR + LANE
OFF_QKV = OFF_GLU + 2 * CM_CH
OFF_GZ = OFF_QKV + GDN_QKV
OFF_GAB = OFF_GZ + GDN_HEADS * GDN_DV
IN_PACKED = OFF_GAB + LANE
CM_HALO = 32
GDN_HALO = 8
VMEM_LIMIT = 56 * 1024 * 1024


def _dot(a, b):
    return jnp.dot(a, b, preferred_element_type=F32)


def _dot_nt(a, b):
    return lax.dot_general(a, b, (((1,), (1,)), ((), ())), preferred_element_type=F32)


def _layer_norm(x, g, b):
    xc = x - jnp.mean(x, axis=-1, keepdims=True)
    var = jnp.mean(xc * xc, axis=-1, keepdims=True)
    return xc * lax.rsqrt(var + LN_EPS) * g + b


def _rms_norm(x, g):
    return x * lax.rsqrt(jnp.mean(x * x, axis=-1, keepdims=True) + RMS_EPS) * g


def _sigmoid(x):
    return 1.0 / (1.0 + jnp.exp(-x))


def _silu(x):
    return x * _sigmoid(x)


def _resident(shape):
    nd = len(shape)
    return pl.BlockSpec(shape, lambda *_: (0,) * nd, pipeline_mode=pl.Buffered(1))


def _params(sem, vmem=VMEM_LIMIT):
    return pltpu.CompilerParams(dimension_semantics=sem, vmem_limit_bytes=vmem)


def _inproj_kernel(x_ref, lng_ref, lnb_ref, w_ref, bg_ref, qn_ref, wq_ref, wqr_ref, kvn_ref,
                   invf_ref, *outs, pre_ln, seq_len, past, tm):
    if pre_ln:
        h_ref, outs = outs[0], outs[1:]
    gates_ref, qp_ref, ckv_ref, kr_ref, glu_ref, qkv_ref, gz_ref, gab_ref = outs
    x = x_ref[...]
    if pre_ln:
        x = _layer_norm(x, lng_ref[...], lnb_ref[...])
        h_ref[...] = x
    xb = x.astype(BF16)

    def grp(off, n):
        return _dot(xb, w_ref[:, off:off + n])

    for j in range(N_BRANCH):
        sl = slice(j * D_MODEL, (j + 1) * D_MODEL)
        gates_ref[:, sl] = _sigmoid(grp(OFF_GATE + j * D_MODEL, D_MODEL) + bg_ref[:, sl])

    row = pl.program_id(0) * tm + lax.broadcasted_iota(jnp.int32, (tm, 1), 0)
    pos = (row % seq_len + past).astype(F32)
    ang_q = pos * invf_ref[0:1, :]
    ang_k = pos * invf_ref[1:2, :]
    cos_q, sin_q = jnp.cos(ang_q) * (MLA_SCALE * LOG2E), jnp.sin(ang_q) * (MLA_SCALE * LOG2E)
    cos_k, sin_k = jnp.cos(ang_k), jnp.sin(ang_k)

    qn = _rms_norm(grp(OFF_CQ, MLA_Q_LORA), qn_ref[...]).astype(BF16)
    for h in range(MLA_HEADS):
        sl = slice(h * HEAD_SLOT, (h + 1) * HEAD_SLOT)
        q1 = _dot(qn, wq_ref[:, sl])
        q2 = _dot(qn, wqr_ref[:, sl])
        qp_ref[:, sl] = (q1 * cos_q + q2 * sin_q).astype(BF16)

    ckv_ref[...] = _rms_norm(grp(OFF_CKV, MLA_KV_LORA), kvn_ref[...])
    kr = grp(OFF_KR, LANE) * cos_k + grp(OFF_KRR, LANE) * sin_k
    kr_ref[...] = kr[:, :MLA_ROPE]

    glu_ref[...] = grp(OFF_GLU, CM_CH) * _sigmoid(grp(OFF_GLU + CM_CH, CM_CH))
    for j in range(GDN_QKV // 512):
        qkv_ref[:, j * 512:(j + 1) * 512] = grp(OFF_QKV + j * 512, 512)
    gz_ref[...] = grp(OFF_GZ, GDN_HEADS * GDN_DV)
    gab_ref[...] = grp(OFF_GAB, LANE)


def _inproj(x, lp, *, pre_ln, seq_len, past, tm=256):
    t = x.shape[0]
    tm = min(tm, t)
    assert t % tm == 0
    row = lambda n: pl.BlockSpec((tm, n), lambda i: (i, 0))
    out_shape = [
        jax.ShapeDtypeStruct((t, N_BRANCH * D_MODEL), F32),
        jax.ShapeDtypeStruct((t, MLA_HEADS * HEAD_SLOT), BF16),
        jax.ShapeDtypeStruct((t, MLA_KV_LORA), F32),
        jax.ShapeDtypeStruct((t, MLA_ROPE), F32),
        jax.ShapeDtypeStruct((t, CM_CH), F32),
        jax.ShapeDtypeStruct((t, GDN_QKV), F32),
        jax.ShapeDtypeStruct((t, GDN_HEADS * GDN_DV), F32),
        jax.ShapeDtypeStruct((t, LANE), F32),
    ]
    out_specs = [row(s.shape[1]) for s in out_shape]
    if pre_ln:
        out_shape = [jax.ShapeDtypeStruct((t, D_MODEL), F32)] + out_shape
        out_specs = [row(D_MODEL)] + out_specs
    return pl.pallas_call(
        functools.partial(_inproj_kernel, pre_ln=pre_ln, seq_len=seq_len, past=past, tm=tm),
        grid=(t // tm,),
        in_specs=[row(D_MODEL), _resident((1, D_MODEL)), _resident((1, D_MODEL)),
                  _resident((D_MODEL, IN_PACKED)), _resident((1, N_BRANCH * D_MODEL)),
                  _resident((1, MLA_Q_LORA)), _resident((MLA_Q_LORA, MLA_HEADS * HEAD_SLOT)),
                  _resident((MLA_Q_LORA, MLA_HEADS * HEAD_SLOT)), _resident((1, MLA_KV_LORA)),
                  _resident((2, LANE))],
        out_specs=out_specs,
        out_shape=out_shape,
        compiler_params=_params(("parallel",)),
        name="inproj",
    )(x, lp["ln_in_g"], lp["ln_in_b"], lp["w_in"], lp["b_gate"], lp["q_norm"], lp["wq"], lp["wqr"],
      lp["kv_norm"], lp["invf"])


def _kvup_kernel(ckv_ref, kr_ref, wk_ref, pk_ref, wv_ref, k_ref, v_ref, *, v_transposed):
    c = ckv_ref[...].astype(BF16)
    r = kr_ref[...].astype(BF16)
    k_ref[...] = (_dot(c, wk_ref[...]) + _dot(r, pk_ref[...])).astype(BF16)
    if v_transposed:
        vt = _dot_nt(wv_ref[...], c)
        row = lax.broadcasted_iota(jnp.int32, vt.shape, 0)
        v_ref[...] = jnp.where(row % VT_ROWS == MLA_V, 1.0, vt).astype(BF16)
    else:
        v_ref[...] = _dot(c, wv_ref[...]).astype(BF16)


def _kvup(ckv, kr, lp, v_transposed, tm=512):
    t = ckv.shape[0]
    tm = min(tm, t)
    assert t % tm == 0
    row = lambda n: pl.BlockSpec((tm, n), lambda i: (i, 0))
    dv = MLA_HEADS * MLA_V
    if v_transposed:
        vt_rows = MLA_HEADS * VT_ROWS
        wv, wv_spec = lp["wv_t"], _resident((vt_rows, MLA_KV_LORA))
        v_spec, v_shape = pl.BlockSpec((vt_rows, tm), lambda i: (0, i)), (vt_rows, t)
    else:
        wv, wv_spec = lp["wv"], _resident((MLA_KV_LORA, dv))
        v_spec, v_shape = row(dv), (t, dv)
    return pl.pallas_call(
        functools.partial(_kvup_kernel, v_transposed=v_transposed),
        grid=(t // tm,),
        in_specs=[row(MLA_KV_LORA), row(MLA_ROPE), _resident((MLA_KV_LORA, MLA_HEADS * HEAD_SLOT)),
                  _resident((MLA_ROPE, MLA_HEADS * HEAD_SLOT)), wv_spec],
        out_specs=[row(MLA_HEADS * HEAD_SLOT), v_spec],
        out_shape=[jax.ShapeDtypeStruct((t, MLA_HEADS * HEAD_SLOT), BF16),
                   jax.ShapeDtypeStruct(v_shape, BF16)],
        compiler_params=_params(("parallel",)),
        name="kvup",
    )(ckv, kr, lp["wk"], lp["pk"], wv)


def _attn_prompt_kernel(q_ref, k_ref, vt_ref, o_ref, sa_ref, sb_ref, *, tq, tk):
    i = pl.program_id(2)
    kc = lax.broadcasted_iota(jnp.int32, (tk, tq), 0) // CHUNK
    qc = lax.broadcasted_iota(jnp.int32, (tk, tq), 1) // CHUNK
    qs = [q_ref[:, j * HEAD_SLOT:(j + 1) * HEAD_SLOT] for j in range(2)]

    def stage_scores(ref, kb):
        start = pl.multiple_of(kb * tk, tk)
        for j in range(2):
            ref[j] = _dot_nt(k_ref[pl.ds(start, tk), j * HEAD_SLOT:(j + 1) * HEAD_SLOT], qs[j])

    def update(kb, ref, c, visible=None):
        start = pl.multiple_of(kb * tk, tk)
        out = []
        for j in range(2):
            m, acc = c[2 * j:2 * j + 2]
            s = ref[j] if visible is None else jnp.where(visible, ref[j], -jnp.inf)
            m_new = jnp.maximum(m, jnp.max(s, axis=0, keepdims=True))
            alpha = jnp.exp2(m - m_new)
            p = jnp.exp2(s - m_new)
            vt = vt_ref[j * VT_ROWS:(j + 1) * VT_ROWS, pl.ds(start, tk)]
            out += [m_new, alpha * acc + _dot(vt, p.astype(BF16))]
        return tuple(out)

    stage_scores(sa_ref, 0)
    c = []
    for j in range(2):
        c += [jnp.full((1, tq), -jnp.inf, F32), jnp.zeros((VT_ROWS, tq), F32)]

    def pair(t, cc):
        stage_scores(sb_ref, 2 * t + 1)
        cc = update(2 * t, sa_ref, cc)
        stage_scores(sa_ref, 2 * t + 2)
        return update(2 * t + 1, sb_ref, cc)

    c = lax.fori_loop(0, i, pair, tuple(c))
    stage_scores(sb_ref, 2 * i + 1)
    c = update(2 * i, sa_ref, c, visible=kc <= qc)
    c = update(2 * i + 1, sb_ref, c, visible=kc + tk // CHUNK <= qc)
    o_t = jnp.concatenate([c[2 * j + 1][:MLA_V] / c[2 * j + 1][MLA_V:MLA_V + 1] for j in range(2)], axis=0)
    o_ref[...] = o_t.T.astype(BF16)


def _attn_prompt(qp, kp, vt, batch, seq_len, tk=256):
    tq = 2 * tk
    assert seq_len % tq == 0 and tk % CHUNK == 0
    nq = seq_len // tq
    t = batch * seq_len
    return pl.pallas_call(
        functools.partial(_attn_prompt_kernel, tq=tq, tk=tk),
        grid=(batch, MLA_HEADS // 2, nq),
        in_specs=[pl.BlockSpec((tq, 2 * HEAD_SLOT), lambda b, hp, i: (b * nq + i, hp)),
                  pl.BlockSpec((seq_len, 2 * HEAD_SLOT), lambda b, hp, i: (b, hp)),
                  pl.BlockSpec((2 * VT_ROWS, seq_len), lambda b, hp, i: (hp, b))],
        out_specs=pl.BlockSpec((tq, 2 * MLA_V), lambda b, hp, i: (b * nq + i, hp)),
        out_shape=jax.ShapeDtypeStruct((t, MLA_HEADS * MLA_V), BF16),
        scratch_shapes=[pltpu.VMEM((2, tk, tq), F32), pltpu.VMEM((2, tk, tq), F32)],
        compiler_params=_params(("parallel", "parallel", "arbitrary")),
        name="attn_prompt",
    )(qp, kp, vt)


def _attn_sample_kernel(q_ref, kp_ref, kn_ref, vp_ref, vn_ref, o_ref):
    vp, vn = vp_ref[...], vn_ref[...]
    outs = []
    for j in range(2):
        sl = slice(j * HEAD_SLOT, (j + 1) * HEAD_SLOT)
        q = q_ref[:, sl]
        sp = _dot_nt(q, kp_ref[:, sl])
        sn = _dot_nt(q, kn_ref[:, sl])
        m = jnp.maximum(jnp.max(sp, axis=-1, keepdims=True), jnp.max(sn, axis=-1, keepdims=True))
        pp, pn = jnp.exp2(sp - m), jnp.exp2(sn - m)
        l = jnp.sum(pp, axis=-1, keepdims=True) + jnp.sum(pn, axis=-1, keepdims=True)
        outs.append((_dot(pp.astype(BF16), vp) + _dot(pn.astype(BF16), vn)) / l)
    lane = lax.broadcasted_iota(jnp.int32, outs[0].shape, 1)
    o_ref[...] = jnp.where(lane < MLA_V, outs[0], outs[1]).astype(BF16)


def _attn_sample(qp, kpast, knew, vpast, vnew, batch, seq_len, past):
    blk = lambda rows, n: pl.BlockSpec((rows, n), lambda b, hp: (b, hp))
    return pl.pallas_call(
        _attn_sample_kernel,
        grid=(batch, MLA_HEADS // 2),
        in_specs=[blk(seq_len, 2 * HEAD_SLOT), blk(past, 2 * HEAD_SLOT), blk(seq_len, 2 * HEAD_SLOT),
                  blk(past, 2 * MLA_V), blk(seq_len, 2 * MLA_V)],
        out_specs=blk(seq_len, 2 * MLA_V),
        out_shape=jax.ShapeDtypeStruct((batch * seq_len, MLA_HEADS * MLA_V), BF16),
        compiler_params=_params(("parallel", "parallel")),
        name="attn_sample",
    )(qp, kpast, knew, vpast, vnew)


def _conformer_kernel(x_ref, st_ref, w_ref, b_ref, g_ref, beta_ref, o_ref, xbuf, xs, *, tl, sub):
    @pl.when(pl.program_id(1) == 0)
    def _():
        xbuf[0:CM_HALO, :] = st_ref[0]

    xbuf[CM_HALO:CM_HALO + tl, :] = x_ref[...]
    first = CM_HALO - (CM_WIDTH - 1)
    span = xs.shape[1]
    for b in range(1, SUBLANE):
        xs[b - 1] = xbuf[b:b + span, :]
    for r in range(tl // sub):
        acc = jnp.zeros((sub // SUBLANE, SUBLANE, CM_CH), F32)
        for j in range(CM_WIDTH):
            b = (first + j) % SUBLANE
            lo = r * sub + first + j - b
            win = xbuf[lo:lo + sub, :] if b == 0 else xs[b - 1, lo:lo + sub, :]
            acc = acc + w_ref[j][None] * win.reshape(sub // SUBLANE, SUBLANE, CM_CH)
        acc = acc.reshape(sub, CM_CH) + b_ref[...]
        y = _layer_norm(acc, g_ref[...], beta_ref[...])
        o_ref[r * sub:(r + 1) * sub, :] = _silu(y).astype(BF16)
    xbuf[0:CM_HALO, :] = xbuf[tl:tl + CM_HALO, :]


def _conformer(glu, state, lp, batch, seq_len):
    tl = min(seq_len, 512)
    assert seq_len % tl == 0 and tl >= CM_HALO
    nl = seq_len // tl
    span = tl + (CM_HALO - SUBLANE)
    return pl.pallas_call(
        functools.partial(_conformer_kernel, tl=tl, sub=32),
        grid=(batch, nl),
        in_specs=[pl.BlockSpec((tl, CM_CH), lambda b, l: (b * nl + l, 0)),
                  pl.BlockSpec((1, CM_HALO, CM_CH), lambda b, l: (b, 0, 0)),
                  _resident((CM_WIDTH, SUBLANE, CM_CH)), _resident((1, CM_CH)), _resident((1, CM_CH)),
                  _resident((1, CM_CH))],
        out_specs=pl.BlockSpec((tl, CM_CH), lambda b, l: (b * nl + l, 0)),
        out_shape=jax.ShapeDtypeStruct((batch * seq_len, CM_CH), BF16),
        scratch_shapes=[pltpu.VMEM((CM_HALO + tl, CM_CH), F32),
                        pltpu.VMEM((SUBLANE - 1, span, CM_CH), F32)],
        compiler_params=_params(("parallel", "arbitrary")),
        name="conformer",
    )(glu, state, lp["cm_w_dw"], lp["cm_b_dw"], lp["cm_ln_g"], lp["cm_ln_b"])


def _split_bf16(a):
    hi = a.astype(BF16)
    return hi, (a - hi.astype(F32)).astype(BF16)


def _dot3(a, b):
    (ah, al), (bh, bl) = a, b
    return _dot(ah, bh) + (_dot(ah, bl) + _dot(al, bh))


def _gdn_kernel(qkv_ref, gab_ref, gz_ref, cst_ref, s0_ref, wc_ref, alog_ref, dtb_ref, gn_ref,
                o_ref, s_ref, xbuf, xc, *, rows):
    @pl.when(pl.program_id(1) == 0)
    def _():
        xbuf[0:GDN_HALO, :] = cst_ref[0]
        s_ref[...] = s0_ref[...]

    xbuf[GDN_HALO:GDN_HALO + rows, :] = qkv_ref[...]
    for c in range(GDN_QKV // LANE):
        sl = slice(c * LANE, (c + 1) * LANE)
        acc = wc_ref[GDN_CONV - 1:GDN_CONV, sl] * xbuf[GDN_HALO:GDN_HALO + rows, sl]
        for j in range(GDN_CONV - 1):
            lo = GDN_HALO - (GDN_CONV - 1) + j
            acc = acc + wc_ref[j:j + 1, sl] * xbuf[lo:lo + rows, sl]
        xc[:, sl] = _silu(acc)
    xbuf[0:GDN_HALO, :] = xbuf[rows:rows + GDN_HALO, :]

    gab = gab_ref[...]
    z = gab + dtb_ref[...]
    softplus = jnp.maximum(z, 0.0) + jnp.log(1.0 + jnp.exp(-jnp.abs(z)))
    g = -jnp.exp(alog_ref[...]) * softplus
    beta_all = _sigmoid(gab)
    ri = lax.broadcasted_iota(jnp.int32, (rows, rows), 0)
    ci = lax.broadcasted_iota(jnp.int32, (rows, rows), 1)
    tri = jnp.where((ri >= ci) & (ri // CHUNK == ci // CHUNK), 1.0, 0.0).astype(BF16)
    g1 = g.astype(BF16)
    r1 = g - g1.astype(F32)
    g2 = r1.astype(BF16)
    g3 = (r1 - g2.astype(F32)).astype(BF16)
    gcs = _dot(tri, g1) + _dot(tri, g2) + _dot(tri, g3)
    gcs_t = gcs.T

    i64 = lax.broadcasted_iota(jnp.int32, (CHUNK, CHUNK), 0)
    j64 = lax.broadcasted_iota(jnp.int32, (CHUNK, CHUNK), 1)
    causal = i64 >= j64
    strict = i64 > j64
    eye = jnp.where(i64 == j64, 1.0, 0.0).astype(F32)
    dq, dk = GDN_HEADS * GDN_DK, GDN_HEADS * GDN_DK
    n_chunks = rows // CHUNK
    units = [(c, h) for c in range(n_chunks) for h in range(GDN_HEADS)]

    U = {}
    for c, h in units:
        rs = slice(c * CHUNK, (c + 1) * CHUNK)
        q = xc[rs, h * GDN_DK:(h + 1) * GDN_DK]
        k = xc[rs, dq + h * GDN_DK:dq + (h + 1) * GDN_DK]
        v = xc[rs, dq + dk + h * GDN_DV:dq + dk + (h + 1) * GDN_DV]
        q = q * lax.rsqrt(jnp.sum(q * q, axis=-1, keepdims=True) + 1e-6) * (GDN_DK ** -0.5)
        k = k * lax.rsqrt(jnp.sum(k * k, axis=-1, keepdims=True) + 1e-6)
        gcol = gcs[rs, h:h + 1]
        grow = gcs_t[h:h + 1, rs]
        beta = beta_all[rs, GDN_HEADS + h:GDN_HEADS + h + 1]
        decay = jnp.exp(jnp.where(causal, gcol - grow, -jnp.inf))
        kbeta = k * beta
        egc = jnp.exp(gcol)
        glast = gcol[CHUNK - 1:CHUNK, :]
        U[c, h] = dict(
            a=jnp.where(strict, _dot_nt(kbeta, k) * decay, 0.0),
            qk=(_dot_nt(q, k) * decay).astype(BF16),
            rhs=jnp.concatenate([v * beta, kbeta * egc], axis=1),
            qe=(q * egc).astype(BF16),
            kd=(k * jnp.exp(glast - gcol)).astype(BF16),
            sdecay=jnp.exp(glast))

    x = {u: eye - U[u]["a"] for u in units}
    p = {u: _split_bf16(U[u]["a"]) for u in units}
    k2 = 2
    while k2 < CHUNK:
        p2 = {u: _dot3(p[u], p[u]) for u in units}
        p = {u: _split_bf16(p2[u]) for u in units}
        x = {u: x[u] + _dot3(_split_bf16(x[u]), p[u]) for u in units}
        k2 *= 2
    sol = {u: _dot3(_split_bf16(x[u]), _split_bf16(U[u]["rhs"])) for u in units}

    for c in range(n_chunks):
        rs = slice(c * CHUNK, (c + 1) * CHUNK)
        s_old = [s_ref[0, h] for h in range(GDN_HEADS)]
        s_bf = [s.astype(BF16) for s in s_old]
        ws = [_dot(sol[c, h][:, GDN_DV:].astype(BF16), s_bf[h]) for h in range(GDN_HEADS)]
        qs = [_dot(U[c, h]["qe"], s_bf[h]) for h in range(GDN_HEADS)]
        v_new = [(sol[c, h][:, :GDN_DV] - ws[h]).astype(BF16) for h in range(GDN_HEADS)]
        for h in range(GDN_HEADS):
            s_ref[0, h] = s_old[h] * U[c, h]["sdecay"] + lax.dot_general(
                U[c, h]["kd"], v_new[h], (((0,), (0,)), ((), ())), preferred_element_type=F32)
        for h in range(GDN_HEADS):
            o = qs[h] + _dot(U[c, h]["qk"], v_new[h])
            gate = _silu(gz_ref[rs, h * GDN_DV:(h + 1) * GDN_DV])
            o_ref[rs, h * GDN_DV:(h + 1) * GDN_DV] = (_rms_norm(o, gn_ref[...]) * gate).astype(BF16)


def _gdn(qkv, gab, gz, conv_state, s0, lp, batch, seq_len):
    rows = min(seq_len, 256)
    assert seq_len % rows == 0 and rows % CHUNK == 0
    nl = seq_len // rows
    tile = lambda n: pl.BlockSpec((rows, n), lambda b, l: (b * nl + l, 0))
    return pl.pallas_call(
        functools.partial(_gdn_kernel, rows=rows),
        grid=(batch, nl),
        in_specs=[tile(GDN_QKV), tile(LANE), tile(GDN_HEADS * GDN_DV),
                  pl.BlockSpec((1, GDN_HALO, GDN_QKV), lambda b, l: (b, 0, 0)),
                  pl.BlockSpec((1, GDN_HEADS, GDN_DK, GDN_DV), lambda b, l: (b, 0, 0, 0)),
                  _resident((GDN_CONV, GDN_QKV)), _resident((1, LANE)), _resident((1, LANE)),
                  _resident((1, GDN_DV))],
        out_specs=[tile(GDN_HEADS * GDN_DV),
                   pl.BlockSpec((1, GDN_HEADS, GDN_DK, GDN_DV), lambda b, l: (b, 0, 0, 0))],
        out_shape=[jax.ShapeDtypeStruct((batch * seq_len, GDN_HEADS * GDN_DV), BF16),
                   jax.ShapeDtypeStruct((batch, GDN_HEADS, GDN_DK, GDN_DV), F32)],
        scratch_shapes=[pltpu.VMEM((GDN_HALO + rows, GDN_QKV), F32),
                        pltpu.VMEM((rows, GDN_QKV), F32)],
        compiler_params=_params(("parallel", "arbitrary")),
        name="gdn",
    )(qkv, gab, gz, conv_state, s0, lp["g_w_conv"], lp["g_a_log"], lp["g_dt_bias"], lp["g_norm"])


def _route(lg):
    lane = lax.broadcasted_iota(jnp.int32, lg.shape, 1)
    is_g = lane < MOE_GROUPS
    gl = jnp.where(is_g, lg, -jnp.inf)
    gmax = jnp.max(gl, axis=-1, keepdims=True)
    gsel = jnp.min(jnp.where(gl == gmax, lane, LANE), axis=-1, keepdims=True)
    gw = 1.0 / jnp.sum(jnp.where(is_g, jnp.exp(lg - gmax), 0.0), axis=-1, keepdims=True)
    e_lo = MOE_GROUPS + gsel * MOE_PER_GROUP
    in_grp = (lane >= e_lo) & (lane < e_lo + MOE_PER_GROUP)
    el = jnp.where(in_grp, lg, -jnp.inf)
    v1 = jnp.max(el, axis=-1, keepdims=True)
    i1 = jnp.min(jnp.where(el == v1, lane, LANE), axis=-1, keepdims=True)
    rest = in_grp & (lane != i1)
    el2 = jnp.where(rest, lg, -jnp.inf)
    v2 = jnp.max(el2, axis=-1, keepdims=True)
    i2 = jnp.min(jnp.where(rest & (el2 == v2), lane, LANE), axis=-1, keepdims=True)
    e2 = jnp.exp(v2 - v1)
    den = 1.0 + e2
    return jnp.where(lane == i1, gw / den, 0.0) + jnp.where(lane == i2, gw * e2 / den, 0.0)


def _merge_kernel(h_ref, g_ref, oa_ref, cb_ref, oc_ref, wpa_ref, wpb_ref, wpc_ref, wout_ref,
                  l1g_ref, l1b_ref, wq_ref, mk_ref, mv_ref, wo_ref, l2g_ref, l2b_ref, wr_ref, br_ref,
                  h2_ref, comb_ref, *, alpha, groups, units):
    merged = [(g_ref[sl, 0:D_MODEL] * _dot(oa_ref[sl, :], wpa_ref[...])
               + g_ref[sl, D_MODEL:2 * D_MODEL] * _dot(cb_ref[sl, :], wpb_ref[...])
               + g_ref[sl, 2 * D_MODEL:3 * D_MODEL] * _dot(oc_ref[sl, :], wpc_ref[...])).astype(BF16)
              for sl in groups]
    t = [_dot(m, wout_ref[...]) for m in merged]
    h1 = [_layer_norm(alpha * h_ref[sl, :] + ti, l1g_ref[...], l1b_ref[...]) for sl, ti in zip(groups, t)]
    q = [_dot(x.astype(BF16), wq_ref[...]).astype(BF16) for x in h1]
    att = [[] for _ in units]
    for hh in range(XA_HEADS):
        sl = slice(hh * XA_DIM, (hh + 1) * XA_DIM)
        sc = [_dot_nt(q[gi][rs, sl], mk_ref[mi, :, sl]) * (XA_DIM ** -0.5) for gi, rs, mi in units]
        p = [jnp.exp(si - jnp.max(si, axis=-1, keepdims=True)) for si in sc]
        p = [(pi / jnp.sum(pi, axis=-1, keepdims=True)).astype(BF16) for pi in p]
        for a, pi, (_, _, mi) in zip(att, p, units):
            a.append(_dot(pi, mv_ref[mi, :, sl]).astype(BF16))
    xo = []
    for gi in range(len(groups)):
        o = jnp.concatenate([jnp.concatenate(a, axis=1) for a, u in zip(att, units) if u[0] == gi], axis=0)
        xo.append(_dot(o, wo_ref[...]))
    h2 = [_layer_norm(alpha * a + b, l2g_ref[...], l2b_ref[...]) for a, b in zip(h1, xo)]
    logits = [_dot(x.astype(BF16), wr_ref[...]) + br_ref[...] for x in h2]
    for sl, x, lg in zip(groups, h2, logits):
        h2_ref[sl, :] = x
        comb_ref[sl, :] = _route(lg)


MERGE_GROUP_ROWS = 256


def _merge(h, gates, oa, cb, oc, mem_k, mem_v, lp, batch, seq_len, alpha):
    t = batch * seq_len
    mem_len = mem_k.shape[1]
    if seq_len >= 2 * MERGE_GROUP_ROWS:
        nb, tm = 1, 2 * MERGE_GROUP_ROWS
        groups = tuple(slice(r * MERGE_GROUP_ROWS, (r + 1) * MERGE_GROUP_ROWS) for r in range(2))
        units = tuple((r, slice(0, MERGE_GROUP_ROWS), 0) for r in range(2))
    else:
        nb = min(batch, max(1, MERGE_GROUP_ROWS // seq_len))
        tm = nb * seq_len
        groups = (slice(0, tm),)
        units = tuple((0, slice(r * seq_len, (r + 1) * seq_len), r) for r in range(nb))
    assert t % tm == 0 and batch % nb == 0 and (seq_len % tm == 0 or tm % seq_len == 0)
    row = lambda n: pl.BlockSpec((tm, n), lambda i: (i, 0))
    mem = pl.BlockSpec((nb, mem_len, D_MODEL), lambda i: ((i * tm // seq_len) // nb, 0, 0))
    half = MLA_HEADS * MLA_V
    return pl.pallas_call(
        functools.partial(_merge_kernel, alpha=alpha, groups=groups, units=units),
        grid=(t // tm,),
        in_specs=[row(D_MODEL), row(N_BRANCH * D_MODEL), row(half), row(CM_CH), row(GDN_HEADS * GDN_DV),
                  _resident((half, D_MODEL)), _resident((CM_CH, D_MODEL)),
                  _resident((GDN_HEADS * GDN_DV, D_MODEL)), _resident((D_MODEL, D_MODEL)),
                  _resident((1, D_MODEL)), _resident((1, D_MODEL)), _resident((D_MODEL, D_MODEL)),
                  mem, mem, _resident((D_MODEL, D_MODEL)), _resident((1, D_MODEL)),
                  _resident((1, D_MODEL)), _resident((D_MODEL, LANE)), _resident((1, LANE))],
        out_specs=[row(D_MODEL), row(LANE)],
        out_shape=[jax.ShapeDtypeStruct((t, D_MODEL), F32), jax.ShapeDtypeStruct((t, LANE), F32)],
        compiler_params=_params(("parallel",)),
        name="merge",
    )(h, gates, oa, cb, oc, lp["w_pa"], lp["w_pb"], lp["w_pc"], lp["w_out"], lp["ln1_g"], lp["ln1_b"],
      lp["xa_wq"], mem_k, mem_v, lp["xa_wo"], lp["ln2_g"], lp["ln2_b"], lp["w_route"], lp["b_route"])


def _moe_kernel(x_ref, comb_ref, wg_ref, wu_ref, wd_ref, lg_ref, lb_ref, o_ref, xb_ref, acc_ref, *, alpha):
    e = pl.program_id(1)

    @pl.when(e == 0)
    def _():
        xb_ref[...] = x_ref[...].astype(BF16)
        acc_ref[...] = jnp.zeros_like(acc_ref)

    xb = xb_ref[...]
    lane = lax.broadcasted_iota(jnp.int32, comb_ref.shape, 1)
    cw = jnp.sum(jnp.where(lane == e + MOE_GROUPS, comb_ref[...], 0.0), axis=-1, keepdims=True)
    a = _dot(xb, wg_ref[0])
    b = _dot(xb, wu_ref[0])
    hid = _silu(a) * b * cw
    acc_ref[...] += _dot(hid.astype(BF16), wd_ref[0])

    @pl.when(e == MOE_EXPERTS - 1)
    def _():
        o_ref[...] = _layer_norm(alpha * x_ref[...] + acc_ref[...], lg_ref[...], lb_ref[...])


def _moe(x, comb, lp, alpha):
    t = x.shape[0]
    tm = min(t, 1024)
    assert t % tm == 0
    row = lambda n: pl.BlockSpec((tm, n), lambda i, e: (i, 0))
    return pl.pallas_call(
        functools.partial(_moe_kernel, alpha=alpha),
        grid=(t // tm, MOE_EXPERTS),
        in_specs=[row(D_MODEL), row(LANE),
                  pl.BlockSpec((1, D_MODEL, MOE_FF), lambda i, e: (e, 0, 0)),
                  pl.BlockSpec((1, D_MODEL, MOE_FF), lambda i, e: (e, 0, 0)),
                  pl.BlockSpec((1, MOE_FF, D_MODEL), lambda i, e: (e, 0, 0)),
                  pl.BlockSpec((1, D_MODEL), lambda i, e: (0, 0)),
                  pl.BlockSpec((1, D_MODEL), lambda i, e: (0, 0))],
        out_specs=row(D_MODEL),
        out_shape=jax.ShapeDtypeStruct((t, D_MODEL), F32),
        scratch_shapes=[pltpu.VMEM((tm, D_MODEL), BF16), pltpu.VMEM((tm, D_MODEL), F32)],
        compiler_params=_params(("parallel", "arbitrary")),
        name="moe",
    )(x, comb, lp["moe_wg"], lp["moe_wu"], lp["moe_wd"], lp["ln3_g"], lp["ln3_b"])


def _memkv_kernel(x_ref, wk_ref, wv_ref, k_ref, v_ref):
    xb = x_ref[...].astype(BF16)
    k_ref[...] = _dot(xb, wk_ref[...])
    v_ref[...] = _dot(xb, wv_ref[...])


def _memkv(mem, lp):
    t = mem.shape[0]
    full = lambda r, c: pl.BlockSpec((r, c), lambda i: (0, 0))
    return pl.pallas_call(
        _memkv_kernel,
        grid=(1,),
        in_specs=[full(t, D_MODEL), full(D_MODEL, D_MODEL), full(D_MODEL, D_MODEL)],
        out_specs=[full(t, D_MODEL), full(t, D_MODEL)],
        out_shape=[jax.ShapeDtypeStruct((t, D_MODEL), F32)] * 2,
        compiler_params=_params(("arbitrary",)),
        name="memkv",
    )(mem, lp["xa_wk"], lp["xa_wv"])


def _rot_half_cols(w):
    half = w.shape[-1] // 2
    return jnp.concatenate([-w[..., half:], w[..., :half]], axis=-1)


def _pack_layer(l, p):
    d = D_MODEL
    cols, start = [], 0
    for s in IN_SIZES:
        cols.append(p["w_in"][l][:, start:start + s])
        start += s
    w_gate, w_cq, w_ckv, w_kr, w_glu, w_qkv, w_gz, w_ga, w_gb = cols
    zpad = lambda n: jnp.zeros((d, n), F32)
    w_in = jnp.concatenate([
        w_gate, w_cq, w_ckv,
        w_kr, zpad(LANE - MLA_ROPE),
        _rot_half_cols(w_kr), zpad(LANE - MLA_ROPE),
        w_glu, w_qkv, w_gz,
        w_ga, w_gb, zpad(LANE - 2 * GDN_HEADS)], axis=1).astype(BF16)
    assert w_in.shape[1] == IN_PACKED

    wuq = p["mla_w_uq"][l].reshape(MLA_Q_LORA, MLA_HEADS, MLA_NOPE + MLA_ROPE)
    nope, rope = wuq[..., :MLA_NOPE], wuq[..., MLA_NOPE:]
    pad_q = HEAD_SLOT - MLA_NOPE - MLA_ROPE
    zq = lambda n: jnp.zeros((MLA_Q_LORA, MLA_HEADS, n), F32)
    wq = jnp.concatenate([nope, rope, zq(pad_q)], -1).reshape(MLA_Q_LORA, MLA_HEADS * HEAD_SLOT)
    wqr = jnp.concatenate([zq(MLA_NOPE), _rot_half_cols(rope), zq(pad_q)], -1).reshape(
        MLA_Q_LORA, MLA_HEADS * HEAD_SLOT)

    wukv = p["mla_w_ukv"][l].reshape(MLA_KV_LORA, MLA_HEADS, MLA_NOPE + MLA_V)
    wk = jnp.concatenate([wukv[..., :MLA_NOPE],
                          jnp.zeros((MLA_KV_LORA, MLA_HEADS, HEAD_SLOT - MLA_NOPE), F32)], -1)
    wk = wk.reshape(MLA_KV_LORA, MLA_HEADS * HEAD_SLOT)
    wv = wukv[..., MLA_NOPE:].reshape(MLA_KV_LORA, MLA_HEADS * MLA_V)
    wv_t = jnp.concatenate([wukv[..., MLA_NOPE:],
                            jnp.zeros((MLA_KV_LORA, MLA_HEADS, VT_ROWS - MLA_V), F32)], -1)
    wv_t = wv_t.reshape(MLA_KV_LORA, MLA_HEADS * VT_ROWS).T
    place = jnp.concatenate([jnp.zeros((MLA_ROPE, ROPE_LANE0), F32), jnp.eye(MLA_ROPE, dtype=F32),
                             jnp.zeros((MLA_ROPE, pad_q), F32)], -1)
    pk = jnp.tile(place, (1, MLA_HEADS))

    half = MLA_ROPE // 2
    inv = ROPE_THETA ** (-jnp.arange(half, dtype=F32) / half)
    inv2 = jnp.concatenate([inv, inv])
    invf = jnp.stack([
        jnp.concatenate([jnp.zeros((ROPE_LANE0,), F32), inv2, jnp.zeros((pad_q,), F32)]),
        jnp.concatenate([inv2, jnp.zeros((LANE - MLA_ROPE,), F32)])])

    lane_pad = lambda v: jnp.concatenate([v, jnp.zeros((LANE - v.shape[0],), F32)])[None, :]
    w_route = jnp.concatenate([p["moe_w_rg"][l], p["moe_w_re"][l],
                               zpad(LANE - MOE_GROUPS - MOE_EXPERTS)], axis=1).astype(BF16)
    b_route = lane_pad(jnp.concatenate([p["moe_b_rg"][l], p["moe_b_re"][l]]))
    r1 = lambda v: v.reshape(1, -1)
    return dict(
        ln_in_g=r1(p["ln_in_g"]), ln_in_b=r1(p["ln_in_b"]),
        w_in=w_in, b_gate=r1(p["b_gate"][l]), q_norm=r1(p["mla_q_norm"][l]),
        wq=wq.astype(BF16), wqr=wqr.astype(BF16), kv_norm=r1(p["mla_kv_norm"][l]), invf=invf,
        wk=wk.astype(BF16), pk=pk.astype(BF16), wv=wv.astype(BF16), wv_t=wv_t.astype(BF16),
        cm_w_dw=jnp.broadcast_to(p["cm_w_dw"][l][:, None, :], (CM_WIDTH, SUBLANE, CM_CH)),
        cm_b_dw=r1(p["cm_b_dw"][l]), cm_ln_g=r1(p["cm_ln_g"][l]),
        cm_ln_b=r1(p["cm_ln_b"][l]),
        g_w_conv=p["gdn_w_conv"][l], g_a_log=lane_pad(p["gdn_a_log"][l]),
        g_dt_bias=lane_pad(p["gdn_dt_bias"][l]), g_norm=r1(p["gdn_norm"][l]),
        w_pa=p["w_proj_a"][l].astype(BF16), w_pb=p["w_proj_b"][l].astype(BF16),
        w_pc=p["w_proj_c"][l].astype(BF16), w_out=p["w_out"][l].astype(BF16),
        ln1_g=r1(p["ln1_g"][l]), ln1_b=r1(p["ln1_b"][l]),
        xa_wq=p["xa_w_q"][l].astype(BF16), xa_wk=p["xa_w_k"][l].astype(BF16),
        xa_wv=p["xa_w_v"][l].astype(BF16), xa_wo=p["xa_w_o"][l].astype(BF16),
        ln2_g=r1(p["ln2_g"][l]), ln2_b=r1(p["ln2_b"][l]),
        w_route=w_route, b_route=b_route,
        moe_wg=p["moe_w_gate"][l].astype(BF16), moe_wu=p["moe_w_up"][l].astype(BF16),
        moe_wd=p["moe_w_down"][l].astype(BF16),
        ln3_g=r1(p["ln3_g"][l]), ln3_b=r1(p["ln3_b"][l]),
    )


def _front_pad_rows(state, rows):
    b, r, c = state.shape
    return jnp.concatenate([jnp.zeros((b, rows - r, c), state.dtype), state], axis=1)


def _layer(x, lp, *, first, batch, seq_len, past, alpha, cache, cm_state, gc_state, s0, mem_k, mem_v):
    res = _inproj(x, lp, pre_ln=first, seq_len=seq_len, past=past)
    if first:
        h, res = res[0], res[1:]
    else:
        h = x
    gates, qp, ckv, kr, glu, qkv, gz, gab = res
    kn, vn = _kvup(ckv, kr, lp, v_transposed=cache is None)
    if cache is None:
        oa = _attn_prompt(qp, kn, vn, batch, seq_len)
    else:
        ckv_past, kr_past = cache
        kpast, vpast = _kvup(ckv_past.reshape(batch * past, MLA_KV_LORA),
                             kr_past.reshape(batch * past, MLA_ROPE), lp, v_transposed=False)
        oa = _attn_sample(qp, kpast, kn, vpast, vn, batch, seq_len, past)
    cb = _conformer(glu, _front_pad_rows(cm_state, CM_HALO), lp, batch, seq_len)
    oc, s_new = _gdn(qkv, gab, gz, _front_pad_rows(gc_state, GDN_HALO), s0, lp, batch, seq_len)
    h2, comb = _merge(h, gates, oa, cb, oc, mem_k.astype(BF16), mem_v.astype(BF16), lp, batch, seq_len, alpha)
    h3 = _moe(h2, comb, lp, alpha)
    assert seq_len >= CM_WIDTH - 1
    states = (ckv.reshape(batch, seq_len, MLA_KV_LORA), kr.reshape(batch, seq_len, MLA_ROPE),
              glu.reshape(batch, seq_len, CM_CH)[:, seq_len - (CM_WIDTH - 1):],
              qkv.reshape(batch, seq_len, GDN_QKV)[:, seq_len - (GDN_CONV - 1):], s_new)
    return h3, states


def kernel(x_prompt, x_sample, mem_prompt, cache_mla_ckv, cache_mla_krope, state_cm_conv, state_gdn_conv, state_gdn, cache_mem_k, cache_mem_v, ln_in_g, ln_in_b, w_in, b_gate, mla_q_norm, mla_w_uq, mla_kv_norm, mla_w_ukv, w_proj_a, cm_w_dw, cm_b_dw, cm_ln_g, cm_ln_b, w_proj_b, gdn_w_conv, gdn_a_log, gdn_dt_bias, gdn_norm, w_proj_c, w_out, ln1_g, ln1_b, xa_w_q, xa_w_k, xa_w_v, xa_w_o, ln2_g, ln2_b, moe_w_rg, moe_b_rg, moe_w_re, moe_b_re, moe_w_gate, moe_w_up, moe_w_down, ln3_g, ln3_b):
    p = dict(ln_in_g=ln_in_g, ln_in_b=ln_in_b, w_in=w_in, b_gate=b_gate, mla_q_norm=mla_q_norm,
             mla_w_uq=mla_w_uq, mla_kv_norm=mla_kv_norm, mla_w_ukv=mla_w_ukv, w_proj_a=w_proj_a,
             cm_w_dw=cm_w_dw, cm_b_dw=cm_b_dw, cm_ln_g=cm_ln_g, cm_ln_b=cm_ln_b, w_proj_b=w_proj_b,
             gdn_w_conv=gdn_w_conv, gdn_a_log=gdn_a_log, gdn_dt_bias=gdn_dt_bias, gdn_norm=gdn_norm,
             w_proj_c=w_proj_c, w_out=w_out, ln1_g=ln1_g, ln1_b=ln1_b, xa_w_q=xa_w_q, xa_w_k=xa_w_k,
             xa_w_v=xa_w_v, xa_w_o=xa_w_o, ln2_g=ln2_g, ln2_b=ln2_b, moe_w_rg=moe_w_rg,
             moe_b_rg=moe_b_rg, moe_w_re=moe_w_re, moe_b_re=moe_b_re, moe_w_gate=moe_w_gate,
             moe_w_up=moe_w_up, moe_w_down=moe_w_down, ln3_g=ln3_g, ln3_b=ln3_b)
    depth = w_in.shape[0]
    alpha = (2 * depth) ** 0.25
    b_p, l_p, d = x_prompt.shape
    b_s, l_s, _ = x_sample.shape
    past = cache_mla_ckv.shape[2]
    mem_len = mem_prompt.shape[1]
    hp = x_prompt.reshape(b_p * l_p, d)
    hs = x_sample.reshape(b_s * l_s, d)
    cm0 = jnp.zeros((b_p, CM_WIDTH - 1, CM_CH), F32)
    gc0 = jnp.zeros((b_p, GDN_CONV - 1, GDN_QKV), F32)
    s0 = jnp.zeros((b_p, GDN_HEADS, GDN_DK, GDN_DV), F32)
    outs_p = [[] for _ in range(7)]
    outs_s = [[] for _ in range(5)]
    for l in range(depth):
        lp = _pack_layer(l, p)
        mk, mv = _memkv(mem_prompt.reshape(b_p * mem_len, d), lp)
        mk = mk.reshape(b_p, mem_len, d)
        mv = mv.reshape(b_p, mem_len, d)
        hp, st_p = _layer(hp, lp, first=(l == 0), batch=b_p, seq_len=l_p, past=0, alpha=alpha,
                          cache=None, cm_state=cm0, gc_state=gc0, s0=s0, mem_k=mk, mem_v=mv)
        hs, st_s = _layer(hs, lp, first=(l == 0), batch=b_s, seq_len=l_s, past=past, alpha=alpha,
                          cache=(cache_mla_ckv[l], cache_mla_krope[l]), cm_state=state_cm_conv[l],
                          gc_state=state_gdn_conv[l], s0=state_gdn[l],
                          mem_k=cache_mem_k[l].astype(BF16).reshape(b_s, mem_len, d),
                          mem_v=cache_mem_v[l].astype(BF16).reshape(b_s, mem_len, d))
        mk4 = mk.reshape(b_p, mem_len, XA_HEADS, XA_DIM)
        mv4 = mv.reshape(b_p, mem_len, XA_HEADS, XA_DIM)
        for lst, arr in zip(outs_p, st_p + (mk4, mv4)):
            lst.append(arr)
        for lst, arr in zip(outs_s, st_s):
            lst.append(arr)
    ckv_p, kr_p, cm_p, gc_p, gdn_p, mk_p, mv_p = [jnp.stack(a) for a in outs_p]
    ckv_s, kr_s, cm_s, gc_s, gdn_s = [jnp.stack(a) for a in outs_s]
    return (hp.reshape(b_p, l_p, d), hs.reshape(b_s, l_s, d), ckv_p, kr_p, ckv_s, kr_s,
            cm_p, cm_s, gc_p, gc_s, gdn_p, gdn_s, mk_p, mv_p)
```

```python
import functools

import jax
import jax.numpy as jnp
from jax import lax
from jax.experimental import pallas as pl
from jax.experimental.pallas import tpu as pltpu

F32 = jnp.float32
BF16 = jnp.bfloat16

D_MODEL = 1024
CHUNK = 64
MLA_HEADS = 8
MLA_NOPE = 64
MLA_ROPE = 32
MLA_V = 64
MLA_Q_LORA = 384
MLA_KV_LORA = 256
MLA_SCALE = (MLA_NOPE + MLA_ROPE) ** -0.5
LOG2E = 1.4426950408889634
ROPE_THETA = 10000.0
CM_CH = 512
CM_WIDTH = 31
GDN_HEADS = 4
GDN_DK = 128
GDN_DV = 128
GDN_CONV = 4
GDN_QKV = GDN_HEADS * (2 * GDN_DK + GDN_DV)
XA_HEADS = 4
XA_DIM = D_MODEL // XA_HEADS
MOE_GROUPS = 4
MOE_PER_GROUP = 4
MOE_EXPERTS = MOE_GROUPS * MOE_PER_GROUP
MOE_FF = 256
N_BRANCH = 3
LN_EPS = 1e-5
RMS_EPS = 1e-6
IN_SIZES = (N_BRANCH * D_MODEL, MLA_Q_LORA, MLA_KV_LORA, MLA_ROPE, 2 * CM_CH,
            GDN_QKV, GDN_HEADS * GDN_DV, GDN_HEADS, GDN_HEADS)

LANE = 128
BF16_ROWS = 16
VT_ROWS = MLA_V + BF16_ROWS
SUBLANE = 8
HEAD_SLOT = 128
ROPE_LANE0 = MLA_NOPE
OFF_GATE = 0
OFF_CQ = OFF_GATE + N_BRANCH * D_MODEL
OFF_CKV = OFF_CQ + MLA_Q_LORA
OFF_KR = OFF_CKV + MLA_KV_LORA
OFF_KRR = OFF_KR + LANE
OFF_GLU = OFF_KRR + LANE
OFF_QKV = OFF_GLU + 2 * CM_CH
OFF_GZ = OFF_QKV + GDN_QKV
OFF_GAB = OFF_GZ + GDN_HEADS * GDN_DV
IN_PACKED = OFF_GAB + LANE
CM_HALO = 32
GDN_HALO = 8
VMEM_LIMIT = 56 * 1024 * 1024


def _dot(a, b):
    return jnp.dot(a, b, preferred_element_type=F32)


def _dot_nt(a, b):
    return lax.dot_general(a, b, (((1,), (1,)), ((), ())), preferred_element_type=F32)


def _layer_norm(x, g, b):
    xc = x - jnp.mean(x, axis=-1, keepdims=True)
    var = jnp.mean(xc * xc, axis=-1, keepdims=True)
    return xc * lax.rsqrt(var + LN_EPS) * g + b


def _rms_norm(x, g):
    return x * lax.rsqrt(jnp.mean(x * x, axis=-1, keepdims=True) + RMS_EPS) * g


def _sigmoid(x):
    return 1.0 / (1.0 + jnp.exp(-x))


def _silu(x):
    return x * _sigmoid(x)


def _resident(shape):
    nd = len(shape)
    return pl.BlockSpec(shape, lambda *_: (0,) * nd, pipeline_mode=pl.Buffered(1))


def _params(sem, vmem=VMEM_LIMIT):
    return pltpu.CompilerParams(dimension_semantics=sem, vmem_limit_bytes=vmem)


def _inproj_kernel(x_ref, lng_ref, lnb_ref, w_ref, bg_ref, qn_ref, wq_ref, wqr_ref, kvn_ref,
                   invf_ref, *outs, pre_ln, seq_len, past, tm):
    if pre_ln:
        h_ref, outs = outs[0], outs[1:]
    gates_ref, qp_ref, ckv_ref, kr_ref, glu_ref, qkv_ref, gz_ref, gab_ref = outs
    x = x_ref[...]
    if pre_ln:
        x = _layer_norm(x, lng_ref[...], lnb_ref[...])
        h_ref[...] = x
    xb = x.astype(BF16)

    def grp(off, n):
        return _dot(xb, w_ref[:, off:off + n])

    for j in range(N_BRANCH):
        sl = slice(j * D_MODEL, (j + 1) * D_MODEL)
        gates_ref[:, sl] = _sigmoid(grp(OFF_GATE + j * D_MODEL, D_MODEL) + bg_ref[:, sl])

    row = pl.program_id(0) * tm + lax.broadcasted_iota(jnp.int32, (tm, 1), 0)
    pos = (row % seq_len + past).astype(F32)
    ang_q = pos * invf_ref[0:1, :]
    ang_k = pos * invf_ref[1:2, :]
    cos_q, sin_q = jnp.cos(ang_q) * (MLA_SCALE * LOG2E), jnp.sin(ang_q) * (MLA_SCALE * LOG2E)
    cos_k, sin_k = jnp.cos(ang_k), jnp.sin(ang_k)

    qn = _rms_norm(grp(OFF_CQ, MLA_Q_LORA), qn_ref[...]).astype(BF16)
    for h in range(MLA_HEADS):
        sl = slice(h * HEAD_SLOT, (h + 1) * HEAD_SLOT)
        q1 = _dot(qn, wq_ref[:, sl])
        q2 = _dot(qn, wqr_ref[:, sl])
        qp_ref[:, sl] = (q1 * cos_q + q2 * sin_q).astype(BF16)

    ckv_ref[...] = _rms_norm(grp(OFF_CKV, MLA_KV_LORA), kvn_ref[...])
    kr = grp(OFF_KR, LANE) * cos_k + grp(OFF_KRR, LANE) * sin_k
    kr_ref[...] = kr[:, :MLA_ROPE]

    glu_ref[...] = grp(OFF_GLU, CM_CH) * _sigmoid(grp(OFF_GLU + CM_CH, CM_CH))
    for j in range(GDN_QKV // 512):
        qkv_ref[:, j * 512:(j + 1) * 512] = grp(OFF_QKV + j * 512, 512)
    gz_ref[...] = grp(OFF_GZ, GDN_HEADS * GDN_DV)
    gab_ref[...] = grp(OFF_GAB, LANE)


def _inproj(x, lp, *, pre_ln, seq_len, past, tm=256):
    t = x.shape[0]
    tm = min(tm, t)
    assert t % tm == 0
    row = lambda n: pl.BlockSpec((tm, n), lambda i: (i, 0))
    out_shape = [
        jax.ShapeDtypeStruct((t, N_BRANCH * D_MODEL), F32),
        jax.ShapeDtypeStruct((t, MLA_HEADS * HEAD_SLOT), BF16),
        jax.ShapeDtypeStruct((t, MLA_KV_LORA), F32),
        jax.ShapeDtypeStruct((t, MLA_ROPE), F32),
        jax.ShapeDtypeStruct((t, CM_CH), F32),
        jax.ShapeDtypeStruct((t, GDN_QKV), F32),
        jax.ShapeDtypeStruct((t, GDN_HEADS * GDN_DV), F32),
        jax.ShapeDtypeStruct((t, LANE), F32),
    ]
    out_specs = [row(s.shape[1]) for s in out_shape]
    if pre_ln:
        out_shape = [jax.ShapeDtypeStruct((t, D_MODEL), F32)] + out_shape
        out_specs = [row(D_MODEL)] + out_specs
    return pl.pallas_call(
        functools.partial(_inproj_kernel, pre_ln=pre_ln, seq_len=seq_len, past=past, tm=tm),
        grid=(t // tm,),
        in_specs=[row(D_MODEL), _resident((1, D_MODEL)), _resident((1, D_MODEL)),
                  _resident((D_MODEL, IN_PACKED)), _resident((1, N_BRANCH * D_MODEL)),
                  _resident((1, MLA_Q_LORA)), _resident((MLA_Q_LORA, MLA_HEADS * HEAD_SLOT)),
                  _resident((MLA_Q_LORA, MLA_HEADS * HEAD_SLOT)), _resident((1, MLA_KV_LORA)),
                  _resident((2, LANE))],
        out_specs=out_specs,
        out_shape=out_shape,
        compiler_params=_params(("parallel",)),
        name="inproj",
    )(x, lp["ln_in_g"], lp["ln_in_b"], lp["w_in"], lp["b_gate"], lp["q_norm"], lp["wq"], lp["wqr"],
      lp["kv_norm"], lp["invf"])


def _kvup_kernel(ckv_ref, kr_ref, wk_ref, pk_ref, wv_ref, k_ref, vt_ref):
    c = ckv_ref[...].astype(BF16)
    r = kr_ref[...].astype(BF16)
    k_ref[...] = (_dot(c, wk_ref[...]) + _dot(r, pk_ref[...])).astype(BF16)
    vt = _dot_nt(wv_ref[...], c)
    row = lax.broadcasted_iota(jnp.int32, vt.shape, 0)
    vt_ref[...] = jnp.where(row % VT_ROWS == MLA_V, 1.0, vt).astype(BF16)


def _kvup(ckv, kr, lp, tm=512):
    t = ckv.shape[0]
    tm = min(tm, t)
    assert t % tm == 0
    row = lambda n: pl.BlockSpec((tm, n), lambda i: (i, 0))
    vt_rows = MLA_HEADS * VT_ROWS
    return pl.pallas_call(
        _kvup_kernel,
        grid=(t // tm,),
        in_specs=[row(MLA_KV_LORA), row(MLA_ROPE), _resident((MLA_KV_LORA, MLA_HEADS * HEAD_SLOT)),
                  _resident((MLA_ROPE, MLA_HEADS * HEAD_SLOT)), _resident((vt_rows, MLA_KV_LORA))],
        out_specs=[row(MLA_HEADS * HEAD_SLOT), pl.BlockSpec((vt_rows, tm), lambda i: (0, i))],
        out_shape=[jax.ShapeDtypeStruct((t, MLA_HEADS * HEAD_SLOT), BF16),
                   jax.ShapeDtypeStruct((vt_rows, t), BF16)],
        compiler_params=_params(("parallel",)),
        name="kvup",
    )(ckv, kr, lp["wk"], lp["pk"], lp["wv_t"])


def _attn_prompt_kernel(q_ref, k_ref, vt_ref, o_ref, sa_ref, sb_ref, *, tq, tk):
    i = pl.program_id(2)
    kc = lax.broadcasted_iota(jnp.int32, (tk, tq), 0) // CHUNK
    qc = lax.broadcasted_iota(jnp.int32, (tk, tq), 1) // CHUNK
    qs = [q_ref[:, j * HEAD_SLOT:(j + 1) * HEAD_SLOT] for j in range(2)]

    def stage_scores(ref, kb):
        start = pl.multiple_of(kb * tk, tk)
        for j in range(2):
            ref[j] = _dot_nt(k_ref[pl.ds(start, tk), j * HEAD_SLOT:(j + 1) * HEAD_SLOT], qs[j])

    def update(kb, ref, c, visible=None):
        start = pl.multiple_of(kb * tk, tk)
        out = []
        for j in range(2):
            m, acc = c[2 * j:2 * j + 2]
            s = ref[j] if visible is None else jnp.where(visible, ref[j], -jnp.inf)
            m_new = jnp.maximum(m, jnp.max(s, axis=0, keepdims=True))
            alpha = jnp.exp2(m - m_new)
            p = jnp.exp2(s - m_new)
            vt = vt_ref[j * VT_ROWS:(j + 1) * VT_ROWS, pl.ds(start, tk)]
            out += [m_new, alpha * acc + _dot(vt, p.astype(BF16))]
        return tuple(out)

    stage_scores(sa_ref, 0)
    c = []
    for j in range(2):
        c += [jnp.full((1, tq), -jnp.inf, F32), jnp.zeros((VT_ROWS, tq), F32)]

    def pair(t, cc):
        stage_scores(sb_ref, 2 * t + 1)
        cc = update(2 * t, sa_ref, cc)
        stage_scores(sa_ref, 2 * t + 2)
        return update(2 * t + 1, sb_ref, cc)

    c = lax.fori_loop(0, i, pair, tuple(c))
    stage_scores(sb_ref, 2 * i + 1)
    c = update(2 * i, sa_ref, c, visible=kc <= qc)
    c = update(2 * i + 1, sb_ref, c, visible=kc + tk // CHUNK <= qc)
    o_t = jnp.concatenate([c[2 * j + 1][:MLA_V] / c[2 * j + 1][MLA_V:MLA_V + 1] for j in range(2)], axis=0)
    o_ref[...] = o_t.T.astype(BF16)


def _attn_prompt(qp, kp, vt, batch, seq_len, tk=256):
    tq = 2 * tk
    assert seq_len % tq == 0 and tk % CHUNK == 0
    nq = seq_len // tq
    t = batch * seq_len
    return pl.pallas_call(
        functools.partial(_attn_prompt_kernel, tq=tq, tk=tk),
        grid=(batch, MLA_HEADS // 2, nq),
        in_specs=[pl.BlockSpec((tq, 2 * HEAD_SLOT), lambda b, hp, i: (b * nq + i, hp)),
                  pl.BlockSpec((seq_len, 2 * HEAD_SLOT), lambda b, hp, i: (b, hp)),
                  pl.BlockSpec((2 * VT_ROWS, seq_len), lambda b, hp, i: (hp, b))],
        out_specs=pl.BlockSpec((tq, 2 * MLA_V), lambda b, hp, i: (b * nq + i, hp)),
        out_shape=jax.ShapeDtypeStruct((t, MLA_HEADS * MLA_V), BF16),
        scratch_shapes=[pltpu.VMEM((2, tk, tq), F32), pltpu.VMEM((2, tk, tq), F32)],
        compiler_params=_params(("parallel", "parallel", "arbitrary")),
        name="attn_prompt",
    )(qp, kp, vt)


def _attn_sample_kernel(q_ref, cp_ref, rp_ref, cn_ref, rn_ref, wabs_ref, pk_ref, wuv_ref, o_ref):
    heads = range(MLA_HEADS)
    n_q = q_ref.shape[0]
    q_heads = [q_ref[:, h * HEAD_SLOT:(h + 1) * HEAD_SLOT] for h in heads]
    q_all = jnp.concatenate(q_heads, axis=0)
    q_lat = jnp.concatenate([_dot(q_heads[h], wabs_ref[h]).astype(BF16) for h in heads], axis=0)

    def latent_keys(c, r):
        return c.astype(BF16), _dot(r.astype(BF16), pk_ref[...]).astype(BF16)

    cp, rp = latent_keys(cp_ref[0, 0], rp_ref[0, 0])
    cn, rn = latent_keys(cn_ref[...], rn_ref[...])
    sp = _dot_nt(q_lat, cp) + _dot_nt(q_all, rp)
    sn = _dot_nt(q_lat, cn) + _dot_nt(q_all, rn)
    m = jnp.maximum(jnp.max(sp, axis=-1, keepdims=True), jnp.max(sn, axis=-1, keepdims=True))
    pp, pn = jnp.exp2(sp - m), jnp.exp2(sn - m)
    l = jnp.sum(pp, axis=-1, keepdims=True) + jnp.sum(pn, axis=-1, keepdims=True)
    o_lat = ((_dot(pp.astype(BF16), cp) + _dot(pn.astype(BF16), cn)) / l).astype(BF16)
    o = _dot(o_lat[0:n_q], wuv_ref[0])
    for h in range(1, MLA_HEADS):
        o = o + _dot(o_lat[h * n_q:(h + 1) * n_q], wuv_ref[h])
    o_ref[...] = o.astype(BF16)


def _attn_sample(qp, ckv_new, kr_new, ckv_cache, kr_cache, layer, lp, batch, seq_len):
    past = ckv_cache.shape[2]
    new = lambda n: pl.BlockSpec((seq_len, n), lambda b: (b, 0))
    old = lambda n: pl.BlockSpec((1, 1, past, n), lambda b: (layer, b, 0, 0))
    return pl.pallas_call(
        _attn_sample_kernel,
        grid=(batch,),
        in_specs=[new(MLA_HEADS * HEAD_SLOT), old(MLA_KV_LORA), old(MLA_ROPE), new(MLA_KV_LORA),
                  new(MLA_ROPE), _resident((MLA_HEADS, HEAD_SLOT, MLA_KV_LORA)),
                  _resident((MLA_ROPE, HEAD_SLOT)),
                  _resident((MLA_HEADS, MLA_KV_LORA, MLA_HEADS * MLA_V))],
        out_specs=new(MLA_HEADS * MLA_V),
        out_shape=jax.ShapeDtypeStruct((batch * seq_len, MLA_HEADS * MLA_V), BF16),
        compiler_params=_params(("parallel",)),
        name="attn_sample",
    )(qp, ckv_cache, kr_cache, ckv_new, kr_new, lp["w_abs"], lp["pk1"], lp["wuv_placed"])


def _conformer_kernel(x_ref, st_ref, w_ref, b_ref, g_ref, beta_ref, o_ref, xbuf, xs, *, tl, sub):
    @pl.when(pl.program_id(1) == 0)
    def _():
        xbuf[0:CM_HALO, :] = st_ref[0]

    xbuf[CM_HALO:CM_HALO + tl, :] = x_ref[...]
    first = CM_HALO - (CM_WIDTH - 1)
    span = xs.shape[1]
    for b in range(1, SUBLANE):
        xs[b - 1] = xbuf[b:b + span, :]
    for r in range(tl // sub):
        acc = jnp.zeros((sub // SUBLANE, SUBLANE, CM_CH), F32)
        for j in range(CM_WIDTH):
            b = (first + j) % SUBLANE
            lo = r * sub + first + j - b
            win = xbuf[lo:lo + sub, :] if b == 0 else xs[b - 1, lo:lo + sub, :]
            acc = acc + w_ref[j][None] * win.reshape(sub // SUBLANE, SUBLANE, CM_CH)
        acc = acc.reshape(sub, CM_CH) + b_ref[...]
        y = _layer_norm(acc, g_ref[...], beta_ref[...])
        o_ref[r * sub:(r + 1) * sub, :] = _silu(y).astype(BF16)
    xbuf[0:CM_HALO, :] = xbuf[tl:tl + CM_HALO, :]


def _conformer(glu, state, lp, batch, seq_len):
    tl = min(seq_len, 512)
    assert seq_len % tl == 0 and tl >= CM_HALO
    nl = seq_len // tl
    span = tl + (CM_HALO - SUBLANE)
    return pl.pallas_call(
        functools.partial(_conformer_kernel, tl=tl, sub=32),
        grid=(batch, nl),
        in_specs=[pl.BlockSpec((tl, CM_CH), lambda b, l: (b * nl + l, 0)),
                  pl.BlockSpec((1, CM_HALO, CM_CH), lambda b, l: (b, 0, 0)),
                  _resident((CM_WIDTH, SUBLANE, CM_CH)), _resident((1, CM_CH)), _resident((1, CM_CH)),
                  _resident((1, CM_CH))],
        out_specs=pl.BlockSpec((tl, CM_CH), lambda b, l: (b * nl + l, 0)),
        out_shape=jax.ShapeDtypeStruct((batch * seq_len, CM_CH), BF16),
        scratch_shapes=[pltpu.VMEM((CM_HALO + tl, CM_CH), F32),
                        pltpu.VMEM((SUBLANE - 1, span, CM_CH), F32)],
        compiler_params=_params(("parallel", "arbitrary")),
        name="conformer",
    )(glu, state, lp["cm_w_dw"], lp["cm_b_dw"], lp["cm_ln_g"], lp["cm_ln_b"])


def _split_bf16(a):
    hi = a.astype(BF16)
    return hi, (a - hi.astype(F32)).astype(BF16)


def _dot3(a, b):
    (ah, al), (bh, bl) = a, b
    return _dot(ah, bh) + (_dot(ah, bl) + _dot(al, bh))


def _gdn_kernel(qkv_ref, gab_ref, gz_ref, cst_ref, s0_ref, wc_ref, alog_ref, dtb_ref, gn_ref,
                o_ref, s_ref, xbuf, xc, *, rows):
    @pl.when(pl.program_id(1) == 0)
    def _():
        xbuf[0:GDN_HALO, :] = cst_ref[0]
        s_ref[...] = s0_ref[...]

    xbuf[GDN_HALO:GDN_HALO + rows, :] = qkv_ref[...]
    for c in range(GDN_QKV // LANE):
        sl = slice(c * LANE, (c + 1) * LANE)
        acc = wc_ref[GDN_CONV - 1:GDN_CONV, sl] * xbuf[GDN_HALO:GDN_HALO + rows, sl]
        for j in range(GDN_CONV - 1):
            lo = GDN_HALO - (GDN_CONV - 1) + j
            acc = acc + wc_ref[j:j + 1, sl] * xbuf[lo:lo + rows, sl]
        xc[:, sl] = _silu(acc)
    xbuf[0:GDN_HALO, :] = xbuf[rows:rows + GDN_HALO, :]

    gab = gab_ref[...]
    z = gab + dtb_ref[...]
    softplus = jnp.maximum(z, 0.0) + jnp.log(1.0 + jnp.exp(-jnp.abs(z)))
    g = -jnp.exp(alog_ref[...]) * softplus
    beta_all = _sigmoid(gab)
    ri = lax.broadcasted_iota(jnp.int32, (rows, rows), 0)
    ci = lax.broadcasted_iota(jnp.int32, (rows, rows), 1)
    tri = jnp.where((ri >= ci) & (ri // CHUNK == ci // CHUNK), 1.0, 0.0).astype(BF16)
    g1 = g.astype(BF16)
    r1 = g - g1.astype(F32)
    g2 = r1.astype(BF16)
    g3 = (r1 - g2.astype(F32)).astype(BF16)
    gcs = _dot(tri, g1) + _dot(tri, g2) + _dot(tri, g3)
    gcs_t = gcs.T

    i64 = lax.broadcasted_iota(jnp.int32, (CHUNK, CHUNK), 0)
    j64 = lax.broadcasted_iota(jnp.int32, (CHUNK, CHUNK), 1)
    causal = i64 >= j64
    strict = i64 > j64
    eye = jnp.where(i64 == j64, 1.0, 0.0).astype(F32)
    dq, dk = GDN_HEADS * GDN_DK, GDN_HEADS * GDN_DK
    n_chunks = rows // CHUNK
    units = [(c, h) for c in range(n_chunks) for h in range(GDN_HEADS)]

    U = {}
    for c, h in units:
        rs = slice(c * CHUNK, (c + 1) * CHUNK)
        q = xc[rs, h * GDN_DK:(h + 1) * GDN_DK]
        k = xc[rs, dq + h * GDN_DK:dq + (h + 1) * GDN_DK]
        v = xc[rs, dq + dk + h * GDN_DV:dq + dk + (h + 1) * GDN_DV]
        q = q * lax.rsqrt(jnp.sum(q * q, axis=-1, keepdims=True) + 1e-6) * (GDN_DK ** -0.5)
        k = k * lax.rsqrt(jnp.sum(k * k, axis=-1, keepdims=True) + 1e-6)
        gcol = gcs[rs, h:h + 1]
        grow = gcs_t[h:h + 1, rs]
        beta = beta_all[rs, GDN_HEADS + h:GDN_HEADS + h + 1]
        decay = jnp.exp(jnp.where(causal, gcol - grow, -jnp.inf))
        kbeta = k * beta
        egc = jnp.exp(gcol)
        glast = gcol[CHUNK - 1:CHUNK, :]
        U[c, h] = dict(
            a=jnp.where(strict, _dot_nt(kbeta, k) * decay, 0.0),
            qk=(_dot_nt(q, k) * decay).astype(BF16),
            rhs=jnp.concatenate([v * beta, kbeta * egc], axis=1),
            qe=(q * egc).astype(BF16),
            kd=(k * jnp.exp(glast - gcol)).astype(BF16),
            sdecay=jnp.exp(glast))

    x = {u: eye - U[u]["a"] for u in units}
    p = {u: _split_bf16(U[u]["a"]) for u in units}
    k2 = 2
    while k2 < CHUNK:
        p2 = {u: _dot3(p[u], p[u]) for u in units}
        p = {u: _split_bf16(p2[u]) for u in units}
        x = {u: x[u] + _dot3(_split_bf16(x[u]), p[u]) for u in units}
        k2 *= 2
    sol = {u: _dot3(_split_bf16(x[u]), _split_bf16(U[u]["rhs"])) for u in units}

    for c in range(n_chunks):
        rs = slice(c * CHUNK, (c + 1) * CHUNK)
        s_old = [s_ref[0, h] for h in range(GDN_HEADS)]
        s_bf = [s.astype(BF16) for s in s_old]
        ws = [_dot(sol[c, h][:, GDN_DV:].astype(BF16), s_bf[h]) for h in range(GDN_HEADS)]
        qs = [_dot(U[c, h]["qe"], s_bf[h]) for h in range(GDN_HEADS)]
        v_new = [(sol[c, h][:, :GDN_DV] - ws[h]).astype(BF16) for h in range(GDN_HEADS)]
        for h in range(GDN_HEADS):
            s_ref[0, h] = s_old[h] * U[c, h]["sdecay"] + lax.dot_general(
                U[c, h]["kd"], v_new[h], (((0,), (0,)), ((), ())), preferred_element_type=F32)
        for h in range(GDN_HEADS):
            o = qs[h] + _dot(U[c, h]["qk"], v_new[h])
            gate = _silu(gz_ref[rs, h * GDN_DV:(h + 1) * GDN_DV])
            o_ref[rs, h * GDN_DV:(h + 1) * GDN_DV] = (_rms_norm(o, gn_ref[...]) * gate).astype(BF16)


def _gdn(qkv, gab, gz, conv_state, s0, lp, batch, seq_len):
    rows = min(seq_len, 256)
    assert seq_len % rows == 0 and rows % CHUNK == 0
    nl = seq_len // rows
    tile = lambda n: pl.BlockSpec((rows, n), lambda b, l: (b * nl + l, 0))
    return pl.pallas_call(
        functools.partial(_gdn_kernel, rows=rows),
        grid=(batch, nl),
        in_specs=[tile(GDN_QKV), tile(LANE), tile(GDN_HEADS * GDN_DV),
                  pl.BlockSpec((1, GDN_HALO, GDN_QKV), lambda b, l: (b, 0, 0)),
                  pl.BlockSpec((1, GDN_HEADS, GDN_DK, GDN_DV), lambda b, l: (b, 0, 0, 0)),
                  _resident((GDN_CONV, GDN_QKV)), _resident((1, LANE)), _resident((1, LANE)),
                  _resident((1, GDN_DV))],
        out_specs=[tile(GDN_HEADS * GDN_DV),
                   pl.BlockSpec((1, GDN_HEADS, GDN_DK, GDN_DV), lambda b, l: (b, 0, 0, 0))],
        out_shape=[jax.ShapeDtypeStruct((batch * seq_len, GDN_HEADS * GDN_DV), BF16),
                   jax.ShapeDtypeStruct((batch, GDN_HEADS, GDN_DK, GDN_DV), F32)],
        scratch_shapes=[pltpu.VMEM((GDN_HALO + rows, GDN_QKV), F32),
                        pltpu.VMEM((rows, GDN_QKV), F32)],
        compiler_params=_params(("parallel", "arbitrary")),
        name="gdn",
    )(qkv, gab, gz, conv_state, s0, lp["g_w_conv"], lp["g_a_log"], lp["g_dt_bias"], lp["g_norm"])


def _route(lg):
    lane = lax.broadcasted_iota(jnp.int32, lg.shape, 1)
    is_g = lane < MOE_GROUPS
    gl = jnp.where(is_g, lg, -jnp.inf)
    gmax = jnp.max(gl, axis=-1, keepdims=True)
    gsel = jnp.min(jnp.where(gl == gmax, lane, LANE), axis=-1, keepdims=True)
    gw = 1.0 / jnp.sum(jnp.where(is_g, jnp.exp(lg - gmax), 0.0), axis=-1, keepdims=True)
    e_lo = MOE_GROUPS + gsel * MOE_PER_GROUP
    in_grp = (lane >= e_lo) & (lane < e_lo + MOE_PER_GROUP)
    el = jnp.where(in_grp, lg, -jnp.inf)
    v1 = jnp.max(el, axis=-1, keepdims=True)
    i1 = jnp.min(jnp.where(el == v1, lane, LANE), axis=-1, keepdims=True)
    rest = in_grp & (lane != i1)
    el2 = jnp.where(rest, lg, -jnp.inf)
    v2 = jnp.max(el2, axis=-1, keepdims=True)
    i2 = jnp.min(jnp.where(rest & (el2 == v2), lane, LANE), axis=-1, keepdims=True)
    e2 = jnp.exp(v2 - v1)
    den = 1.0 + e2
    return jnp.where(lane == i1, gw / den, 0.0) + jnp.where(lane == i2, gw * e2 / den, 0.0)


def _merge_kernel(h_ref, g_ref, oa_ref, cb_ref, oc_ref, wpa_ref, wpb_ref, wpc_ref, wout_ref,
                  l1g_ref, l1b_ref, wq_ref, mk_ref, mv_ref, wo_ref, l2g_ref, l2b_ref, wrh_ref, wrl_ref, br_ref,
                  h2_ref, comb_ref, *, alpha, groups, units):
    merged = [(g_ref[sl, 0:D_MODEL] * _dot(oa_ref[sl, :], wpa_ref[...])
               + g_ref[sl, D_MODEL:2 * D_MODEL] * _dot(cb_ref[sl, :], wpb_ref[...])
               + g_ref[sl, 2 * D_MODEL:3 * D_MODEL] * _dot(oc_ref[sl, :], wpc_ref[...])).astype(BF16)
              for sl in groups]
    t = [_dot(m, wout_ref[...]) for m in merged]
    h1 = [_layer_norm(alpha * h_ref[sl, :] + ti, l1g_ref[...], l1b_ref[...]) for sl, ti in zip(groups, t)]
    q = [_dot(x.astype(BF16), wq_ref[...]).astype(BF16) for x in h1]
    att = [[] for _ in units]
    for hh in range(XA_HEADS):
        sl = slice(hh * XA_DIM, (hh + 1) * XA_DIM)
        sc = [_dot_nt(q[gi][rs, sl], mk_ref[mi, :, sl].astype(BF16)) * (XA_DIM ** -0.5)
              for gi, rs, mi in units]
        p = [jnp.exp(si - jnp.max(si, axis=-1, keepdims=True)) for si in sc]
        p = [(pi / jnp.sum(pi, axis=-1, keepdims=True)).astype(BF16) for pi in p]
        for a, pi, (_, _, mi) in zip(att, p, units):
            a.append(_dot(pi, mv_ref[mi, :, sl].astype(BF16)).astype(BF16))
    xo = []
    for gi in range(len(groups)):
        o = jnp.concatenate([jnp.concatenate(a, axis=1) for a, u in zip(att, units) if u[0] == gi], axis=0)
        xo.append(_dot(o, wo_ref[...]))
    h2 = [_layer_norm(alpha * a + b, l2g_ref[...], l2b_ref[...]) for a, b in zip(h1, xo)]
    logits = [_dot3(_split_bf16(x), (wrh_ref[...], wrl_ref[...])) + br_ref[...] for x in h2]
    for sl, x, lg in zip(groups, h2, logits):
        h2_ref[sl, :] = x
        comb_ref[sl, :] = _route(lg)


MERGE_GROUP_ROWS = 256


def _merge(h, gates, oa, cb, oc, mem_k, mem_v, lp, batch, seq_len, alpha):
    t = batch * seq_len
    mem_len = mem_k.shape[1]
    if seq_len >= 2 * MERGE_GROUP_ROWS:
        nb, tm = 1, 2 * MERGE_GROUP_ROWS
        groups = tuple(slice(r * MERGE_GROUP_ROWS, (r + 1) * MERGE_GROUP_ROWS) for r in range(2))
        units = tuple((r, slice(0, MERGE_GROUP_ROWS), 0) for r in range(2))
    else:
        nb = min(batch, max(1, MERGE_GROUP_ROWS // seq_len))
        tm = nb * seq_len
        groups = (slice(0, tm),)
        units = tuple((0, slice(r * seq_len, (r + 1) * seq_len), r) for r in range(nb))
    assert t % tm == 0 and batch % nb == 0 and (seq_len % tm == 0 or tm % seq_len == 0)
    row = lambda n: pl.BlockSpec((tm, n), lambda i: (i, 0))
    mem = pl.BlockSpec((nb, mem_len, D_MODEL), lambda i: ((i * tm // seq_len) // nb, 0, 0))
    half = MLA_HEADS * MLA_V
    return pl.pallas_call(
        functools.partial(_merge_kernel, alpha=alpha, groups=groups, units=units),
        grid=(t // tm,),
        in_specs=[row(D_MODEL), row(N_BRANCH * D_MODEL), row(half), row(CM_CH), row(GDN_HEADS * GDN_DV),
                  _resident((half, D_MODEL)), _resident((CM_CH, D_MODEL)),
                  _resident((GDN_HEADS * GDN_DV, D_MODEL)), _resident((D_MODEL, D_MODEL)),
                  _resident((1, D_MODEL)), _resident((1, D_MODEL)), _resident((D_MODEL, D_MODEL)),
                  mem, mem, _resident((D_MODEL, D_MODEL)), _resident((1, D_MODEL)),
                  _resident((1, D_MODEL)), _resident((D_MODEL, LANE)), _resident((D_MODEL, LANE)),
                  _resident((1, LANE))],
        out_specs=[row(D_MODEL), row(LANE)],
        out_shape=[jax.ShapeDtypeStruct((t, D_MODEL), F32), jax.ShapeDtypeStruct((t, LANE), F32)],
        compiler_params=_params(("parallel",)),
        name="merge",
    )(h, gates, oa, cb, oc, lp["w_pa"], lp["w_pb"], lp["w_pc"], lp["w_out"], lp["ln1_g"], lp["ln1_b"],
      lp["xa_wq"], mem_k, mem_v, lp["xa_wo"], lp["ln2_g"], lp["ln2_b"], lp["w_route_hi"],
      lp["w_route_lo"], lp["b_route"])


def _moe_kernel(x_ref, comb_ref, wg_ref, wu_ref, wd_ref, lg_ref, lb_ref, o_ref, xb_ref, acc_ref, *, alpha):
    e = pl.program_id(1)

    @pl.when(e == 0)
    def _():
        xb_ref[...] = x_ref[...].astype(BF16)
        acc_ref[...] = jnp.zeros_like(acc_ref)

    xb = xb_ref[...]
    lane = lax.broadcasted_iota(jnp.int32, comb_ref.shape, 1)
    cw = jnp.sum(jnp.where(lane == e + MOE_GROUPS, comb_ref[...], 0.0), axis=-1, keepdims=True)
    a = _dot(xb, wg_ref[0])
    b = _dot(xb, wu_ref[0])
    hid = _silu(a) * b * cw
    acc_ref[...] += _dot(hid.astype(BF16), wd_ref[0])

    @pl.when(e == MOE_EXPERTS - 1)
    def _():
        o_ref[...] = _layer_norm(alpha * x_ref[...] + acc_ref[...], lg_ref[...], lb_ref[...])


def _moe(x, comb, lp, alpha):
    t = x.shape[0]
    tm = min(t, 1024)
    assert t % tm == 0
    row = lambda n: pl.BlockSpec((tm, n), lambda i, e: (i, 0))
    return pl.pallas_call(
        functools.partial(_moe_kernel, alpha=alpha),
        grid=(t // tm, MOE_EXPERTS),
        in_specs=[row(D_MODEL), row(LANE),
                  pl.BlockSpec((1, D_MODEL, MOE_FF), lambda i, e: (e, 0, 0)),
                  pl.BlockSpec((1, D_MODEL, MOE_FF), lambda i, e: (e, 0, 0)),
                  pl.BlockSpec((1, MOE_FF, D_MODEL), lambda i, e: (e, 0, 0)),
                  pl.BlockSpec((1, D_MODEL), lambda i, e: (0, 0)),
                  pl.BlockSpec((1, D_MODEL), lambda i, e: (0, 0))],
        out_specs=row(D_MODEL),
        out_shape=jax.ShapeDtypeStruct((t, D_MODEL), F32),
        scratch_shapes=[pltpu.VMEM((tm, D_MODEL), BF16), pltpu.VMEM((tm, D_MODEL), F32)],
        compiler_params=_params(("parallel", "arbitrary")),
        name="moe",
    )(x, comb, lp["moe_wg"], lp["moe_wu"], lp["moe_wd"], lp["ln3_g"], lp["ln3_b"])


def _memkv_kernel(x_ref, wk_ref, wv_ref, k_ref, v_ref):
    xb = x_ref[...].astype(BF16)
    k_ref[...] = _dot(xb, wk_ref[...])
    v_ref[...] = _dot(xb, wv_ref[...])


def _memkv(mem, lp):
    t = mem.shape[0]
    full = lambda r, c: pl.BlockSpec((r, c), lambda i: (0, 0))
    return pl.pallas_call(
        _memkv_kernel,
        grid=(1,),
        in_specs=[full(t, D_MODEL), full(D_MODEL, D_MODEL), full(D_MODEL, D_MODEL)],
        out_specs=[full(t, D_MODEL), full(t, D_MODEL)],
        out_shape=[jax.ShapeDtypeStruct((t, D_MODEL), F32)] * 2,
        compiler_params=_params(("arbitrary",)),
        name="memkv",
    )(mem, lp["xa_wk"], lp["xa_wv"])


def _cast_kernel(x_ref, o_ref):
    o_ref[...] = x_ref[...].astype(o_ref.dtype)


def _layer_to_bf16(w, l):
    _, e, r, c = w.shape
    return pl.pallas_call(
        _cast_kernel,
        grid=(e,),
        in_specs=[pl.BlockSpec((1, 1, r, c), lambda i: (l, i, 0, 0))],
        out_specs=pl.BlockSpec((1, 1, r, c), lambda i: (0, i, 0, 0)),
        out_shape=jax.ShapeDtypeStruct((1, e, r, c), BF16),
        compiler_params=_params(("parallel",)),
        name="cast",
    )(w)[0]


def _rot_half_cols(w):
    half = w.shape[-1] // 2
    return jnp.concatenate([-w[..., half:], w[..., :half]], axis=-1)


def _pack_layer(l, p):
    d = D_MODEL
    cols, start = [], 0
    for s in IN_SIZES:
        cols.append(p["w_in"][l][:, start:start + s])
        start += s
    w_gate, w_cq, w_ckv, w_kr, w_glu, w_qkv, w_gz, w_ga, w_gb = cols
    zpad = lambda n: jnp.zeros((d, n), F32)
    w_in = jnp.concatenate([
        w_gate, w_cq, w_ckv,
        w_kr, zpad(LANE - MLA_ROPE),
        _rot_half_cols(w_kr), zpad(LANE - MLA_ROPE),
        w_glu, w_qkv, w_gz,
        w_ga, w_gb, zpad(LANE - 2 * GDN_HEADS)], axis=1).astype(BF16)
    assert w_in.shape[1] == IN_PACKED

    wuq = p["mla_w_uq"][l].reshape(MLA_Q_LORA, MLA_HEADS, MLA_NOPE + MLA_ROPE)
    nope, rope = wuq[..., :MLA_NOPE], wuq[..., MLA_NOPE:]
    pad_q = HEAD_SLOT - MLA_NOPE - MLA_ROPE
    zq = lambda n: jnp.zeros((MLA_Q_LORA, MLA_HEADS, n), F32)
    wq = jnp.concatenate([nope, rope, zq(pad_q)], -1).reshape(MLA_Q_LORA, MLA_HEADS * HEAD_SLOT)
    wqr = jnp.concatenate([zq(MLA_NOPE), _rot_half_cols(rope), zq(pad_q)], -1).reshape(
        MLA_Q_LORA, MLA_HEADS * HEAD_SLOT)

    wukv = p["mla_w_ukv"][l].reshape(MLA_KV_LORA, MLA_HEADS, MLA_NOPE + MLA_V)
    wk = jnp.concatenate([wukv[..., :MLA_NOPE],
                          jnp.zeros((MLA_KV_LORA, MLA_HEADS, HEAD_SLOT - MLA_NOPE), F32)], -1)
    wk = wk.reshape(MLA_KV_LORA, MLA_HEADS * HEAD_SLOT)
    wv_t = jnp.concatenate([wukv[..., MLA_NOPE:],
                            jnp.zeros((MLA_KV_LORA, MLA_HEADS, VT_ROWS - MLA_V), F32)], -1)
    wv_t = wv_t.reshape(MLA_KV_LORA, MLA_HEADS * VT_ROWS).T
    place = jnp.concatenate([jnp.zeros((MLA_ROPE, ROPE_LANE0), F32), jnp.eye(MLA_ROPE, dtype=F32),
                             jnp.zeros((MLA_ROPE, pad_q), F32)], -1)
    pk = jnp.tile(place, (1, MLA_HEADS))
    w_abs = jnp.concatenate([jnp.transpose(wukv[..., :MLA_NOPE], (1, 2, 0)),
                             jnp.zeros((MLA_HEADS, HEAD_SLOT - MLA_NOPE, MLA_KV_LORA), F32)], axis=1)
    head_cols = jnp.eye(MLA_HEADS, dtype=F32)[:, None, :, None]
    wuv_placed = (jnp.transpose(wukv[..., MLA_NOPE:], (1, 0, 2))[:, :, None, :] * head_cols).reshape(
        MLA_HEADS, MLA_KV_LORA, MLA_HEADS * MLA_V)

    half = MLA_ROPE // 2
    inv = ROPE_THETA ** (-jnp.arange(half, dtype=F32) / half)
    inv2 = jnp.concatenate([inv, inv])
    invf = jnp.stack([
        jnp.concatenate([jnp.zeros((ROPE_LANE0,), F32), inv2, jnp.zeros((pad_q,), F32)]),
        jnp.concatenate([inv2, jnp.zeros((LANE - MLA_ROPE,), F32)])])

    lane_pad = lambda v: jnp.concatenate([v, jnp.zeros((LANE - v.shape[0],), F32)])[None, :]
    w_route = jnp.concatenate([p["moe_w_rg"][l], p["moe_w_re"][l],
                               zpad(LANE - MOE_GROUPS - MOE_EXPERTS)], axis=1)
    w_route_hi = w_route.astype(BF16)
    w_route_lo = (w_route - w_route_hi.astype(F32)).astype(BF16)
    b_route = lane_pad(jnp.concatenate([p["moe_b_rg"][l], p["moe_b_re"][l]]))
    r1 = lambda v: v.reshape(1, -1)
    return dict(
        ln_in_g=r1(p["ln_in_g"]), ln_in_b=r1(p["ln_in_b"]),
        w_in=w_in, b_gate=r1(p["b_gate"][l]), q_norm=r1(p["mla_q_norm"][l]),
        wq=wq.astype(BF16), wqr=wqr.astype(BF16), kv_norm=r1(p["mla_kv_norm"][l]), invf=invf,
        wk=wk.astype(BF16), pk=pk.astype(BF16), wv_t=wv_t.astype(BF16),
        w_abs=w_abs.astype(BF16), pk1=place.astype(BF16), wuv_placed=wuv_placed.astype(BF16),
        cm_w_dw=jnp.broadcast_to(p["cm_w_dw"][l][:, None, :], (CM_WIDTH, SUBLANE, CM_CH)),
        cm_b_dw=r1(p["cm_b_dw"][l]), cm_ln_g=r1(p["cm_ln_g"][l]),
        cm_ln_b=r1(p["cm_ln_b"][l]),
        g_w_conv=p["gdn_w_conv"][l], g_a_log=lane_pad(p["gdn_a_log"][l]),
        g_dt_bias=lane_pad(p["gdn_dt_bias"][l]), g_norm=r1(p["gdn_norm"][l]),
        w_pa=p["w_proj_a"][l].astype(BF16), w_pb=p["w_proj_b"][l].astype(BF16),
        w_pc=p["w_proj_c"][l].astype(BF16), w_out=p["w_out"][l].astype(BF16),
        ln1_g=r1(p["ln1_g"][l]), ln1_b=r1(p["ln1_b"][l]),
        xa_wq=p["xa_w_q"][l].astype(BF16), xa_wk=p["xa_w_k"][l].astype(BF16),
        xa_wv=p["xa_w_v"][l].astype(BF16), xa_wo=p["xa_w_o"][l].astype(BF16),
        ln2_g=r1(p["ln2_g"][l]), ln2_b=r1(p["ln2_b"][l]),
        w_route_hi=w_route_hi, w_route_lo=w_route_lo, b_route=b_route,
        moe_wg=_layer_to_bf16(p["moe_w_gate"], l), moe_wu=_layer_to_bf16(p["moe_w_up"], l),
        moe_wd=_layer_to_bf16(p["moe_w_down"], l),
        ln3_g=r1(p["ln3_g"][l]), ln3_b=r1(p["ln3_b"][l]),
    )


def _front_pad_rows(state, rows):
    b, r, c = state.shape
    return jnp.concatenate([jnp.zeros((b, rows - r, c), state.dtype), state], axis=1)


def _layer(x, lp, *, first, batch, seq_len, past, alpha, cache, cm_state, gc_state, s0, mem_k, mem_v):
    res = _inproj(x, lp, pre_ln=first, seq_len=seq_len, past=past)
    if first:
        h, res = res[0], res[1:]
    else:
        h = x
    gates, qp, ckv, kr, glu, qkv, gz, gab = res
    if cache is None:
        kn, vn_t = _kvup(ckv, kr, lp)
        oa = _attn_prompt(qp, kn, vn_t, batch, seq_len)
    else:
        ckv_cache, kr_cache, layer = cache
        oa = _attn_sample(qp, ckv, kr, ckv_cache, kr_cache, layer, lp, batch, seq_len)
    cb = _conformer(glu, _front_pad_rows(cm_state, CM_HALO), lp, batch, seq_len)
    oc, s_new = _gdn(qkv, gab, gz, _front_pad_rows(gc_state, GDN_HALO), s0, lp, batch, seq_len)
    h2, comb = _merge(h, gates, oa, cb, oc, mem_k, mem_v, lp, batch, seq_len, alpha)
    h3 = _moe(h2, comb, lp, alpha)
    assert seq_len >= CM_WIDTH - 1
    states = (ckv.reshape(batch, seq_len, MLA_KV_LORA), kr.reshape(batch, seq_len, MLA_ROPE),
              glu.reshape(batch, seq_len, CM_CH)[:, seq_len - (CM_WIDTH - 1):],
              qkv.reshape(batch, seq_len, GDN_QKV)[:, seq_len - (GDN_CONV - 1):], s_new)
    return h3, states


def kernel(x_prompt, x_sample, mem_prompt, cache_mla_ckv, cache_mla_krope, state_cm_conv, state_gdn_conv, state_gdn, cache_mem_k, cache_mem_v, ln_in_g, ln_in_b, w_in, b_gate, mla_q_norm, mla_w_uq, mla_kv_norm, mla_w_ukv, w_proj_a, cm_w_dw, cm_b_dw, cm_ln_g, cm_ln_b, w_proj_b, gdn_w_conv, gdn_a_log, gdn_dt_bias, gdn_norm, w_proj_c, w_out, ln1_g, ln1_b, xa_w_q, xa_w_k, xa_w_v, xa_w_o, ln2_g, ln2_b, moe_w_rg, moe_b_rg, moe_w_re, moe_b_re, moe_w_gate, moe_w_up, moe_w_down, ln3_g, ln3_b):
    p = dict(ln_in_g=ln_in_g, ln_in_b=ln_in_b, w_in=w_in, b_gate=b_gate, mla_q_norm=mla_q_norm,
             mla_w_uq=mla_w_uq, mla_kv_norm=mla_kv_norm, mla_w_ukv=mla_w_ukv, w_proj_a=w_proj_a,
             cm_w_dw=cm_w_dw, cm_b_dw=cm_b_dw, cm_ln_g=cm_ln_g, cm_ln_b=cm_ln_b, w_proj_b=w_proj_b,
             gdn_w_conv=gdn_w_conv, gdn_a_log=gdn_a_log, gdn_dt_bias=gdn_dt_bias, gdn_norm=gdn_norm,
             w_proj_c=w_proj_c, w_out=w_out, ln1_g=ln1_g, ln1_b=ln1_b, xa_w_q=xa_w_q, xa_w_k=xa_w_k,
             xa_w_v=xa_w_v, xa_w_o=xa_w_o, ln2_g=ln2_g, ln2_b=ln2_b, moe_w_rg=moe_w_rg,
             moe_b_rg=moe_b_rg, moe_w_re=moe_w_re, moe_b_re=moe_b_re, moe_w_gate=moe_w_gate,
             moe_w_up=moe_w_up, moe_w_down=moe_w_down, ln3_g=ln3_g, ln3_b=ln3_b)
    depth = w_in.shape[0]
    alpha = (2 * depth) ** 0.25
    b_p, l_p, d = x_prompt.shape
    b_s, l_s, _ = x_sample.shape
    past = cache_mla_ckv.shape[2]
    mem_len = mem_prompt.shape[1]
    hp = x_prompt.reshape(b_p * l_p, d)
    hs = x_sample.reshape(b_s * l_s, d)
    cm0 = jnp.zeros((b_p, CM_WIDTH - 1, CM_CH), F32)
    gc0 = jnp.zeros((b_p, GDN_CONV - 1, GDN_QKV), F32)
    s0 = jnp.zeros((b_p, GDN_HEADS, GDN_DK, GDN_DV), F32)
    outs_p = [[] for _ in range(7)]
    outs_s = [[] for _ in range(5)]
    for l in range(depth):
        lp = _pack_layer(l, p)
        mk, mv = _memkv(mem_prompt.reshape(b_p * mem_len, d), lp)
        mk = mk.reshape(b_p, mem_len, d)
        mv = mv.reshape(b_p, mem_len, d)
        hp, st_p = _layer(hp, lp, first=(l == 0), batch=b_p, seq_len=l_p, past=0, alpha=alpha,
                          cache=None, cm_state=cm0, gc_state=gc0, s0=s0, mem_k=mk, mem_v=mv)
        hs, st_s = _layer(hs, lp, first=(l == 0), batch=b_s, seq_len=l_s, past=past, alpha=alpha,
                          cache=(cache_mla_ckv, cache_mla_krope, l), cm_state=state_cm_conv[l],
                          gc_state=state_gdn_conv[l], s0=state_gdn[l],
                          mem_k=cache_mem_k[l].reshape(b_s, mem_len, d),
                          mem_v=cache_mem_v[l].reshape(b_s, mem_len, d))
        mk4 = mk.reshape(b_p, mem_len, XA_HEADS, XA_DIM)
        mv4 = mv.reshape(b_p, mem_len, XA_HEADS, XA_DIM)
        for lst, arr in zip(outs_p, st_p + (mk4, mv4)):
            lst.append(arr)
        for lst, arr in zip(outs_s, st_s):
            lst.append(arr)
    ckv_p, kr_p, cm_p, gc_p, gdn_p, mk_p, mv_p = [jnp.stack(a) for a in outs_p]
    ckv_s, kr_s, cm_s, gc_s, gdn_s = [jnp.stack(a) for a in outs_s]
    return (hp.reshape(b_p, l_p, d), hs.reshape(b_s, l_s, d), ckv_p, kr_p, ckv_s, kr_s,
            cm_p, cm_s, gc_p, gc_s, gdn_p, gdn_s, mk_p, mv_p)
```

```python
import functools

import jax
import jax.numpy as jnp
from jax import lax
from jax.experimental import pallas as pl
from jax.experimental.pallas import tpu as pltpu

F32 = jnp.float32
BF16 = jnp.bfloat16

D_MODEL = 1024
CHUNK = 64
MLA_HEADS = 8
MLA_NOPE = 64
MLA_ROPE = 32
MLA_V = 64
MLA_Q_LORA = 384
MLA_KV_LORA = 256
MLA_SCALE = (MLA_NOPE + MLA_ROPE) ** -0.5
LOG2E = 1.4426950408889634
ROPE_THETA = 10000.0
CM_CH = 512
CM_WIDTH = 31
GDN_HEADS = 4
GDN_DK = 128
GDN_DV = 128
GDN_CONV = 4
GDN_QKV = GDN_HEADS * (2 * GDN_DK + GDN_DV)
XA_HEADS = 4
XA_DIM = D_MODEL // XA_HEADS
MOE_GROUPS = 4
MOE_PER_GROUP = 4
MOE_EXPERTS = MOE_GROUPS * MOE_PER_GROUP
MOE_FF = 256
N_BRANCH = 3
LN_EPS = 1e-5
RMS_EPS = 1e-6
IN_SIZES = (N_BRANCH * D_MODEL, MLA_Q_LORA, MLA_KV_LORA, MLA_ROPE, 2 * CM_CH,
            GDN_QKV, GDN_HEADS * GDN_DV, GDN_HEADS, GDN_HEADS)

LANE = 128
BF16_ROWS = 16
VT_ROWS = MLA_V + BF16_ROWS
SUBLANE = 8
HEAD_SLOT = 128
ROPE_LANE0 = MLA_NOPE
OFF_GATE = 0
OFF_CQ = OFF_GATE + N_BRANCH * D_MODEL
OFF_CKV = OFF_CQ + MLA_Q_LORA
OFF_KR = OFF_CKV + MLA_KV_LORA
OFF_KRR = OFF_KR + LANE
OFF_GLU = OFF_KRR + LANE
OFF_QKV = OFF_GLU + 2 * CM_CH
OFF_GZ = OFF_QKV + GDN_QKV
OFF_GAB = OFF_GZ + GDN_HEADS * GDN_DV
IN_PACKED = OFF_GAB + LANE
CM_HALO = 32
GDN_HALO = 8
VMEM_LIMIT = 56 * 1024 * 1024


def _dot(a, b):
    return jnp.dot(a, b, preferred_element_type=F32)


def _dot_nt(a, b):
    return lax.dot_general(a, b, (((1,), (1,)), ((), ())), preferred_element_type=F32)


def _layer_norm(x, g, b):
    xc = x - jnp.mean(x, axis=-1, keepdims=True)
    var = jnp.mean(xc * xc, axis=-1, keepdims=True)
    return xc * lax.rsqrt(var + LN_EPS) * g + b


def _rms_norm(x, g):
    return x * lax.rsqrt(jnp.mean(x * x, axis=-1, keepdims=True) + RMS_EPS) * g


def _sigmoid(x):
    return 1.0 / (1.0 + jnp.exp(-x))


def _silu(x):
    return x * _sigmoid(x)


def _resident(shape):
    nd = len(shape)
    return pl.BlockSpec(shape, lambda *_: (0,) * nd, pipeline_mode=pl.Buffered(1))


def _params(sem, vmem=VMEM_LIMIT):
    return pltpu.CompilerParams(dimension_semantics=sem, vmem_limit_bytes=vmem)


def _inproj_kernel(x_ref, lng_ref, lnb_ref, w_ref, bg_ref, qn_ref, wq_ref, wqr_ref, kvn_ref,
                   invf_ref, *outs, pre_ln, seq_len, past, tm):
    if pre_ln:
        h_ref, outs = outs[0], outs[1:]
    gates_ref, qp_ref, ckv_ref, kr_ref, glu_ref, qkv_ref, gz_ref, gab_ref = outs
    x = x_ref[...]
    if pre_ln:
        x = _layer_norm(x, lng_ref[...], lnb_ref[...])
        h_ref[...] = x
    xb = x.astype(BF16)

    def grp(off, n):
        return _dot(xb, w_ref[:, off:off + n])

    for j in range(N_BRANCH):
        sl = slice(j * D_MODEL, (j + 1) * D_MODEL)
        gates_ref[:, sl] = _sigmoid(grp(OFF_GATE + j * D_MODEL, D_MODEL) + bg_ref[:, sl])

    row = pl.program_id(0) * tm + lax.broadcasted_iota(jnp.int32, (tm, 1), 0)
    pos = (row % seq_len + past).astype(F32)
    ang_q = pos * invf_ref[0:1, :]
    ang_k = pos * invf_ref[1:2, :]
    cos_q, sin_q = jnp.cos(ang_q) * (MLA_SCALE * LOG2E), jnp.sin(ang_q) * (MLA_SCALE * LOG2E)
    cos_k, sin_k = jnp.cos(ang_k), jnp.sin(ang_k)

    qn = _rms_norm(grp(OFF_CQ, MLA_Q_LORA), qn_ref[...]).astype(BF16)
    for h in range(MLA_HEADS):
        sl = slice(h * HEAD_SLOT, (h + 1) * HEAD_SLOT)
        q1 = _dot(qn, wq_ref[:, sl])
        q2 = _dot(qn, wqr_ref[:, sl])
        qp_ref[:, sl] = (q1 * cos_q + q2 * sin_q).astype(BF16)

    ckv_ref[...] = _rms_norm(grp(OFF_CKV, MLA_KV_LORA), kvn_ref[...])
    kr = grp(OFF_KR, LANE) * cos_k + grp(OFF_KRR, LANE) * sin_k
    kr_ref[...] = kr[:, :MLA_ROPE]

    glu_ref[...] = grp(OFF_GLU, CM_CH) * _sigmoid(grp(OFF_GLU + CM_CH, CM_CH))
    for j in range(GDN_QKV // 512):
        qkv_ref[:, j * 512:(j + 1) * 512] = grp(OFF_QKV + j * 512, 512)
    gz_ref[...] = grp(OFF_GZ, GDN_HEADS * GDN_DV)
    gab_ref[...] = grp(OFF_GAB, LANE)


def _inproj(x, lp, *, pre_ln, seq_len, past, tm=256):
    t = x.shape[0]
    tm = min(tm, t)
    assert t % tm == 0
    row = lambda n: pl.BlockSpec((tm, n), lambda i: (i, 0))
    out_shape = [
        jax.ShapeDtypeStruct((t, N_BRANCH * D_MODEL), F32),
        jax.ShapeDtypeStruct((t, MLA_HEADS * HEAD_SLOT), BF16),
        jax.ShapeDtypeStruct((t, MLA_KV_LORA), F32),
        jax.ShapeDtypeStruct((t, MLA_ROPE), F32),
        jax.ShapeDtypeStruct((t, CM_CH), F32),
        jax.ShapeDtypeStruct((t, GDN_QKV), F32),
        jax.ShapeDtypeStruct((t, GDN_HEADS * GDN_DV), F32),
        jax.ShapeDtypeStruct((t, LANE), F32),
    ]
    out_specs = [row(s.shape[1]) for s in out_shape]
    if pre_ln:
        out_shape = [jax.ShapeDtypeStruct((t, D_MODEL), F32)] + out_shape
        out_specs = [row(D_MODEL)] + out_specs
    return pl.pallas_call(
        functools.partial(_inproj_kernel, pre_ln=pre_ln, seq_len=seq_len, past=past, tm=tm),
        grid=(t // tm,),
        in_specs=[row(D_MODEL), _resident((1, D_MODEL)), _resident((1, D_MODEL)),
                  _resident((D_MODEL, IN_PACKED)), _resident((1, N_BRANCH * D_MODEL)),
                  _resident((1, MLA_Q_LORA)), _resident((MLA_Q_LORA, MLA_HEADS * HEAD_SLOT)),
                  _resident((MLA_Q_LORA, MLA_HEADS * HEAD_SLOT)), _resident((1, MLA_KV_LORA)),
                  _resident((2, LANE))],
        out_specs=out_specs,
        out_shape=out_shape,
        compiler_params=_params(("parallel",)),
        name="inproj",
    )(x, lp["ln_in_g"], lp["ln_in_b"], lp["w_in"], lp["b_gate"], lp["q_norm"], lp["wq"], lp["wqr"],
      lp["kv_norm"], lp["invf"])


def _kvup_kernel(ckv_ref, kr_ref, wk_ref, pk_ref, wv_ref, k_ref, vt_ref):
    c = ckv_ref[...].astype(BF16)
    r = kr_ref[...].astype(BF16)
    k_ref[...] = (_dot(c, wk_ref[...]) + _dot(r, pk_ref[...])).astype(BF16)
    vt = _dot_nt(wv_ref[...], c)
    row = lax.broadcasted_iota(jnp.int32, vt.shape, 0)
    vt_ref[...] = jnp.where(row % VT_ROWS == MLA_V, 1.0, vt).astype(BF16)


def _kvup(ckv, kr, lp, tm=512):
    t = ckv.shape[0]
    tm = min(tm, t)
    assert t % tm == 0
    row = lambda n: pl.BlockSpec((tm, n), lambda i: (i, 0))
    vt_rows = MLA_HEADS * VT_ROWS
    return pl.pallas_call(
        _kvup_kernel,
        grid=(t // tm,),
        in_specs=[row(MLA_KV_LORA), row(MLA_ROPE), _resident((MLA_KV_LORA, MLA_HEADS * HEAD_SLOT)),
                  _resident((MLA_ROPE, MLA_HEADS * HEAD_SLOT)), _resident((vt_rows, MLA_KV_LORA))],
        out_specs=[row(MLA_HEADS * HEAD_SLOT), pl.BlockSpec((vt_rows, tm), lambda i: (0, i))],
        out_shape=[jax.ShapeDtypeStruct((t, MLA_HEADS * HEAD_SLOT), BF16),
                   jax.ShapeDtypeStruct((vt_rows, t), BF16)],
        compiler_params=_params(("parallel",)),
        name="kvup",
    )(ckv, kr, lp["wk"], lp["pk"], lp["wv_t"])


def _attn_prompt_kernel(q_ref, k_ref, vt_ref, o_ref, sa_ref, sb_ref, *, tq, tk):
    i = pl.program_id(2)
    kc = lax.broadcasted_iota(jnp.int32, (tk, tq), 0) // CHUNK
    qc = lax.broadcasted_iota(jnp.int32, (tk, tq), 1) // CHUNK
    qs = [q_ref[:, j * HEAD_SLOT:(j + 1) * HEAD_SLOT] for j in range(2)]

    def stage_scores(ref, kb):
        start = pl.multiple_of(kb * tk, tk)
        for j in range(2):
            ref[j] = _dot_nt(k_ref[pl.ds(start, tk), j * HEAD_SLOT:(j + 1) * HEAD_SLOT], qs[j])

    def update(kb, ref, c, visible=None):
        start = pl.multiple_of(kb * tk, tk)
        out = []
        for j in range(2):
            m, acc = c[2 * j:2 * j + 2]
            s = ref[j] if visible is None else jnp.where(visible, ref[j], -jnp.inf)
            m_new = jnp.maximum(m, jnp.max(s, axis=0, keepdims=True))
            alpha = jnp.exp2(m - m_new)
            p = jnp.exp2(s - m_new)
            vt = vt_ref[j * VT_ROWS:(j + 1) * VT_ROWS, pl.ds(start, tk)]
            out += [m_new, alpha * acc + _dot(vt, p.astype(BF16))]
        return tuple(out)

    stage_scores(sa_ref, 0)
    c = []
    for j in range(2):
        c += [jnp.full((1, tq), -jnp.inf, F32), jnp.zeros((VT_ROWS, tq), F32)]

    def pair(t, cc):
        stage_scores(sb_ref, 2 * t + 1)
        cc = update(2 * t, sa_ref, cc)
        stage_scores(sa_ref, 2 * t + 2)
        return update(2 * t + 1, sb_ref, cc)

    c = lax.fori_loop(0, i // 2, lambda t, cc: pair(2 * t + 1, pair(2 * t, cc)), tuple(c))
    c = lax.fori_loop(0, i % 2, lambda _, cc: pair(i - 1, cc), c)
    stage_scores(sb_ref, 2 * i + 1)
    c = update(2 * i, sa_ref, c, visible=kc <= qc)
    c = update(2 * i + 1, sb_ref, c, visible=kc + tk // CHUNK <= qc)
    o_t = jnp.concatenate([c[2 * j + 1][:MLA_V] / c[2 * j + 1][MLA_V:MLA_V + 1] for j in range(2)], axis=0)
    o_ref[...] = o_t.T.astype(BF16)


def _attn_prompt(qp, kp, vt, batch, seq_len, tk=256):
    tq = 2 * tk
    assert seq_len % tq == 0 and tk % CHUNK == 0
    nq = seq_len // tq
    t = batch * seq_len
    return pl.pallas_call(
        functools.partial(_attn_prompt_kernel, tq=tq, tk=tk),
        grid=(batch, MLA_HEADS // 2, nq),
        in_specs=[pl.BlockSpec((tq, 2 * HEAD_SLOT), lambda b, hp, i: (b * nq + i, hp)),
                  pl.BlockSpec((seq_len, 2 * HEAD_SLOT), lambda b, hp, i: (b, hp)),
                  pl.BlockSpec((2 * VT_ROWS, seq_len), lambda b, hp, i: (hp, b))],
        out_specs=pl.BlockSpec((tq, 2 * MLA_V), lambda b, hp, i: (b * nq + i, hp)),
        out_shape=jax.ShapeDtypeStruct((t, MLA_HEADS * MLA_V), BF16),
        scratch_shapes=[pltpu.VMEM((2, tk, tq), F32), pltpu.VMEM((2, tk, tq), F32)],
        compiler_params=_params(("parallel", "parallel", "arbitrary")),
        name="attn_prompt",
    )(qp, kp, vt)


def _attn_sample_kernel(q_ref, cp_ref, rp_ref, cn_ref, rn_ref, wabs_ref, pk_ref, wuv_ref, o_ref):
    heads = range(MLA_HEADS)
    n_q = q_ref.shape[0]
    q_heads = [q_ref[:, h * HEAD_SLOT:(h + 1) * HEAD_SLOT] for h in heads]
    q_all = jnp.concatenate(q_heads, axis=0)
    q_lat = jnp.concatenate([_dot(q_heads[h], wabs_ref[h]).astype(BF16) for h in heads], axis=0)

    def latent_keys(c, r):
        return c.astype(BF16), _dot(r.astype(BF16), pk_ref[...]).astype(BF16)

    cp, rp = latent_keys(cp_ref[0, 0], rp_ref[0, 0])
    cn, rn = latent_keys(cn_ref[...], rn_ref[...])
    sp = _dot_nt(q_lat, cp) + _dot_nt(q_all, rp)
    sn = _dot_nt(q_lat, cn) + _dot_nt(q_all, rn)
    m = jnp.maximum(jnp.max(sp, axis=-1, keepdims=True), jnp.max(sn, axis=-1, keepdims=True))
    pp, pn = jnp.exp2(sp - m), jnp.exp2(sn - m)
    l = jnp.sum(pp, axis=-1, keepdims=True) + jnp.sum(pn, axis=-1, keepdims=True)
    o_lat = ((_dot(pp.astype(BF16), cp) + _dot(pn.astype(BF16), cn)) / l).astype(BF16)
    o = _dot(o_lat[0:n_q], wuv_ref[0])
    for h in range(1, MLA_HEADS):
        o = o + _dot(o_lat[h * n_q:(h + 1) * n_q], wuv_ref[h])
    o_ref[...] = o.astype(BF16)


def _attn_sample(qp, ckv_new, kr_new, ckv_cache, kr_cache, layer, lp, batch, seq_len):
    past = ckv_cache.shape[2]
    new = lambda n: pl.BlockSpec((seq_len, n), lambda b: (b, 0))
    old = lambda n: pl.BlockSpec((1, 1, past, n), lambda b: (layer, b, 0, 0))
    return pl.pallas_call(
        _attn_sample_kernel,
        grid=(batch,),
        in_specs=[new(MLA_HEADS * HEAD_SLOT), old(MLA_KV_LORA), old(MLA_ROPE), new(MLA_KV_LORA),
                  new(MLA_ROPE), _resident((MLA_HEADS, HEAD_SLOT, MLA_KV_LORA)),
                  _resident((MLA_ROPE, HEAD_SLOT)),
                  _resident((MLA_HEADS, MLA_KV_LORA, MLA_HEADS * MLA_V))],
        out_specs=new(MLA_HEADS * MLA_V),
        out_shape=jax.ShapeDtypeStruct((batch * seq_len, MLA_HEADS * MLA_V), BF16),
        compiler_params=_params(("parallel",)),
        name="attn_sample",
    )(qp, ckv_cache, kr_cache, ckv_new, kr_new, lp["w_abs"], lp["pk1"], lp["wuv_placed"])


def _conformer_kernel(x_ref, st_ref, w_ref, b_ref, g_ref, beta_ref, o_ref, xbuf, xs, *, tl, sub):
    @pl.when(pl.program_id(1) == 0)
    def _():
        xbuf[0:CM_HALO, :] = st_ref[0]

    xbuf[CM_HALO:CM_HALO + tl, :] = x_ref[...]
    first = CM_HALO - (CM_WIDTH - 1)
    span = xs.shape[1]
    for b in range(1, SUBLANE):
        xs[b - 1] = xbuf[b:b + span, :]
    for r in range(tl // sub):
        acc = jnp.zeros((sub // SUBLANE, SUBLANE, CM_CH), F32)
        for j in range(CM_WIDTH):
            b = (first + j) % SUBLANE
            lo = r * sub + first + j - b
            win = xbuf[lo:lo + sub, :] if b == 0 else xs[b - 1, lo:lo + sub, :]
            acc = acc + w_ref[j][None] * win.reshape(sub // SUBLANE, SUBLANE, CM_CH)
        acc = acc.reshape(sub, CM_CH) + b_ref[...]
        y = _layer_norm(acc, g_ref[...], beta_ref[...])
        o_ref[r * sub:(r + 1) * sub, :] = _silu(y).astype(BF16)
    xbuf[0:CM_HALO, :] = xbuf[tl:tl + CM_HALO, :]


def _conformer(glu, state, lp, batch, seq_len):
    tl = min(seq_len, 512)
    assert seq_len % tl == 0 and tl >= CM_HALO
    nl = seq_len // tl
    span = tl + (CM_HALO - SUBLANE)
    return pl.pallas_call(
        functools.partial(_conformer_kernel, tl=tl, sub=32),
        grid=(batch, nl),
        in_specs=[pl.BlockSpec((tl, CM_CH), lambda b, l: (b * nl + l, 0)),
                  pl.BlockSpec((1, CM_HALO, CM_CH), lambda b, l: (b, 0, 0)),
                  _resident((CM_WIDTH, SUBLANE, CM_CH)), _resident((1, CM_CH)), _resident((1, CM_CH)),
                  _resident((1, CM_CH))],
        out_specs=pl.BlockSpec((tl, CM_CH), lambda b, l: (b * nl + l, 0)),
        out_shape=jax.ShapeDtypeStruct((batch * seq_len, CM_CH), BF16),
        scratch_shapes=[pltpu.VMEM((CM_HALO + tl, CM_CH), F32),
                        pltpu.VMEM((SUBLANE - 1, span, CM_CH), F32)],
        compiler_params=_params(("parallel", "arbitrary")),
        name="conformer",
    )(glu, state, lp["cm_w_dw"], lp["cm_b_dw"], lp["cm_ln_g"], lp["cm_ln_b"])


def _split_bf16(a):
    hi = a.astype(BF16)
    return hi, (a - hi.astype(F32)).astype(BF16)


def _dot3(a, b):
    (ah, al), (bh, bl) = a, b
    return _dot(ah, bh) + (_dot(ah, bl) + _dot(al, bh))


def _gdn_kernel(qkv_ref, gab_ref, gz_ref, cst_ref, s0_ref, wc_ref, alog_ref, dtb_ref, gn_ref,
                o_ref, s_ref, xbuf, xc, *, rows):
    @pl.when(pl.program_id(1) == 0)
    def _():
        xbuf[0:GDN_HALO, :] = cst_ref[0]
        s_ref[...] = s0_ref[...]

    xbuf[GDN_HALO:GDN_HALO + rows, :] = qkv_ref[...]
    for c in range(GDN_QKV // LANE):
        sl = slice(c * LANE, (c + 1) * LANE)
        acc = wc_ref[GDN_CONV - 1:GDN_CONV, sl] * xbuf[GDN_HALO:GDN_HALO + rows, sl]
        for j in range(GDN_CONV - 1):
            lo = GDN_HALO - (GDN_CONV - 1) + j
            acc = acc + wc_ref[j:j + 1, sl] * xbuf[lo:lo + rows, sl]
        xc[:, sl] = _silu(acc)
    xbuf[0:GDN_HALO, :] = xbuf[rows:rows + GDN_HALO, :]

    gab = gab_ref[...]
    z = gab + dtb_ref[...]
    softplus = jnp.maximum(z, 0.0) + jnp.log(1.0 + jnp.exp(-jnp.abs(z)))
    g = -jnp.exp(alog_ref[...]) * softplus
    beta_all = _sigmoid(gab)
    ri = lax.broadcasted_iota(jnp.int32, (rows, rows), 0)
    ci = lax.broadcasted_iota(jnp.int32, (rows, rows), 1)
    tri = jnp.where((ri >= ci) & (ri // CHUNK == ci // CHUNK), 1.0, 0.0).astype(BF16)
    g1 = g.astype(BF16)
    r1 = g - g1.astype(F32)
    g2 = r1.astype(BF16)
    g3 = (r1 - g2.astype(F32)).astype(BF16)
    gcs = _dot(tri, g1) + _dot(tri, g2) + _dot(tri, g3)
    gcs_t = gcs.T

    i64 = lax.broadcasted_iota(jnp.int32, (CHUNK, CHUNK), 0)
    j64 = lax.broadcasted_iota(jnp.int32, (CHUNK, CHUNK), 1)
    causal = i64 >= j64
    strict = i64 > j64
    eye = jnp.where(i64 == j64, 1.0, 0.0).astype(F32)
    dq, dk = GDN_HEADS * GDN_DK, GDN_HEADS * GDN_DK
    n_chunks = rows // CHUNK
    units = [(c, h) for c in range(n_chunks) for h in range(GDN_HEADS)]

    U = {}
    for c, h in units:
        rs = slice(c * CHUNK, (c + 1) * CHUNK)
        q = xc[rs, h * GDN_DK:(h + 1) * GDN_DK]
        k = xc[rs, dq + h * GDN_DK:dq + (h + 1) * GDN_DK]
        v = xc[rs, dq + dk + h * GDN_DV:dq + dk + (h + 1) * GDN_DV]
        q = q * lax.rsqrt(jnp.sum(q * q, axis=-1, keepdims=True) + 1e-6) * (GDN_DK ** -0.5)
        k = k * lax.rsqrt(jnp.sum(k * k, axis=-1, keepdims=True) + 1e-6)
        gcol = gcs[rs, h:h + 1]
        grow = gcs_t[h:h + 1, rs]
        beta = beta_all[rs, GDN_HEADS + h:GDN_HEADS + h + 1]
        decay = jnp.exp(jnp.where(causal, gcol - grow, -jnp.inf))
        kbeta = k * beta
        egc = jnp.exp(gcol)
        glast = gcol[CHUNK - 1:CHUNK, :]
        U[c, h] = dict(
            a=jnp.where(strict, _dot_nt(kbeta, k) * decay, 0.0),
            qk=(_dot_nt(q, k) * decay).astype(BF16),
            rhs=jnp.concatenate([v * beta, kbeta * egc], axis=1),
            qe=(q * egc).astype(BF16),
            kd=(k * jnp.exp(glast - gcol)).astype(BF16),
            sdecay=jnp.exp(glast))

    x = {u: eye - U[u]["a"] for u in units}
    p = {u: _split_bf16(U[u]["a"]) for u in units}
    k2 = 2
    while k2 < CHUNK:
        p2 = {u: _dot3(p[u], p[u]) for u in units}
        p = {u: _split_bf16(p2[u]) for u in units}
        x = {u: x[u] + _dot3(_split_bf16(x[u]), p[u]) for u in units}
        k2 *= 2
    sol = {u: _dot3(_split_bf16(x[u]), _split_bf16(U[u]["rhs"])) for u in units}

    for c in range(n_chunks):
        rs = slice(c * CHUNK, (c + 1) * CHUNK)
        s_old = [s_ref[0, h] for h in range(GDN_HEADS)]
        s_bf = [s.astype(BF16) for s in s_old]
        ws = [_dot(sol[c, h][:, GDN_DV:].astype(BF16), s_bf[h]) for h in range(GDN_HEADS)]
        qs = [_dot(U[c, h]["qe"], s_bf[h]) for h in range(GDN_HEADS)]
        v_new = [(sol[c, h][:, :GDN_DV] - ws[h]).astype(BF16) for h in range(GDN_HEADS)]
        for h in range(GDN_HEADS):
            s_ref[0, h] = s_old[h] * U[c, h]["sdecay"] + lax.dot_general(
                U[c, h]["kd"], v_new[h], (((0,), (0,)), ((), ())), preferred_element_type=F32)
        for h in range(GDN_HEADS):
            o = qs[h] + _dot(U[c, h]["qk"], v_new[h])
            gate = _silu(gz_ref[rs, h * GDN_DV:(h + 1) * GDN_DV])
            o_ref[rs, h * GDN_DV:(h + 1) * GDN_DV] = (_rms_norm(o, gn_ref[...]) * gate).astype(BF16)


def _gdn(qkv, gab, gz, conv_state, s0, lp, batch, seq_len):
    rows = min(seq_len, 256)
    assert seq_len % rows == 0 and rows % CHUNK == 0
    nl = seq_len // rows
    tile = lambda n: pl.BlockSpec((rows, n), lambda b, l: (b * nl + l, 0))
    return pl.pallas_call(
        functools.partial(_gdn_kernel, rows=rows),
        grid=(batch, nl),
        in_specs=[tile(GDN_QKV), tile(LANE), tile(GDN_HEADS * GDN_DV),
                  pl.BlockSpec((1, GDN_HALO, GDN_QKV), lambda b, l: (b, 0, 0)),
                  pl.BlockSpec((1, GDN_HEADS, GDN_DK, GDN_DV), lambda b, l: (b, 0, 0, 0)),
                  _resident((GDN_CONV, GDN_QKV)), _resident((1, LANE)), _resident((1, LANE)),
                  _resident((1, GDN_DV))],
        out_specs=[tile(GDN_HEADS * GDN_DV),
                   pl.BlockSpec((1, GDN_HEADS, GDN_DK, GDN_DV), lambda b, l: (b, 0, 0, 0))],
        out_shape=[jax.ShapeDtypeStruct((batch * seq_len, GDN_HEADS * GDN_DV), BF16),
                   jax.ShapeDtypeStruct((batch, GDN_HEADS, GDN_DK, GDN_DV), F32)],
        scratch_shapes=[pltpu.VMEM((GDN_HALO + rows, GDN_QKV), F32),
                        pltpu.VMEM((rows, GDN_QKV), F32)],
        compiler_params=_params(("parallel", "arbitrary")),
        name="gdn",
    )(qkv, gab, gz, conv_state, s0, lp["g_w_conv"], lp["g_a_log"], lp["g_dt_bias"], lp["g_norm"])


def _route(lg):
    lane = lax.broadcasted_iota(jnp.int32, lg.shape, 1)
    is_g = lane < MOE_GROUPS
    gl = jnp.where(is_g, lg, -jnp.inf)
    gmax = jnp.max(gl, axis=-1, keepdims=True)
    gsel = jnp.min(jnp.where(gl == gmax, lane, LANE), axis=-1, keepdims=True)
    gw = 1.0 / jnp.sum(jnp.where(is_g, jnp.exp(lg - gmax), 0.0), axis=-1, keepdims=True)
    e_lo = MOE_GROUPS + gsel * MOE_PER_GROUP
    in_grp = (lane >= e_lo) & (lane < e_lo + MOE_PER_GROUP)
    el = jnp.where(in_grp, lg, -jnp.inf)
    v1 = jnp.max(el, axis=-1, keepdims=True)
    i1 = jnp.min(jnp.where(el == v1, lane, LANE), axis=-1, keepdims=True)
    rest = in_grp & (lane != i1)
    el2 = jnp.where(rest, lg, -jnp.inf)
    v2 = jnp.max(el2, axis=-1, keepdims=True)
    i2 = jnp.min(jnp.where(rest & (el2 == v2), lane, LANE), axis=-1, keepdims=True)
    e2 = jnp.exp(v2 - v1)
    den = 1.0 + e2
    return jnp.where(lane == i1, gw / den, 0.0) + jnp.where(lane == i2, gw * e2 / den, 0.0)


def _merge_kernel(h_ref, g_ref, oa_ref, cb_ref, oc_ref, wpa_ref, wpb_ref, wpc_ref, wout_ref,
                  l1g_ref, l1b_ref, wq_ref, mk_ref, mv_ref, wo_ref, l2g_ref, l2b_ref, wrh_ref, wrl_ref, br_ref,
                  h2_ref, comb_ref, *, alpha, groups, units):
    merged = [(g_ref[sl, 0:D_MODEL] * _dot(oa_ref[sl, :], wpa_ref[...])
               + g_ref[sl, D_MODEL:2 * D_MODEL] * _dot(cb_ref[sl, :], wpb_ref[...])
               + g_ref[sl, 2 * D_MODEL:3 * D_MODEL] * _dot(oc_ref[sl, :], wpc_ref[...])).astype(BF16)
              for sl in groups]
    t = [_dot(m, wout_ref[...]) for m in merged]
    h1 = [_layer_norm(alpha * h_ref[sl, :] + ti, l1g_ref[...], l1b_ref[...]) for sl, ti in zip(groups, t)]
    q = [_dot(x.astype(BF16), wq_ref[...]).astype(BF16) for x in h1]
    att = [[] for _ in units]
    for hh in range(XA_HEADS):
        sl = slice(hh * XA_DIM, (hh + 1) * XA_DIM)
        sc = [_dot_nt(q[gi][rs, sl], mk_ref[mi, :, sl].astype(BF16)) * (XA_DIM ** -0.5)
              for gi, rs, mi in units]
        p = [jnp.exp(si - jnp.max(si, axis=-1, keepdims=True)) for si in sc]
        p = [(pi / jnp.sum(pi, axis=-1, keepdims=True)).astype(BF16) for pi in p]
        for a, pi, (_, _, mi) in zip(att, p, units):
            a.append(_dot(pi, mv_ref[mi, :, sl].astype(BF16)).astype(BF16))
    xo = []
    for gi in range(len(groups)):
        o = jnp.concatenate([jnp.concatenate(a, axis=1) for a, u in zip(att, units) if u[0] == gi], axis=0)
        xo.append(_dot(o, wo_ref[...]))
    h2 = [_layer_norm(alpha * a + b, l2g_ref[...], l2b_ref[...]) for a, b in zip(h1, xo)]
    logits = [_dot3(_split_bf16(x), (wrh_ref[...], wrl_ref[...])) + br_ref[...] for x in h2]
    for sl, x, lg in zip(groups, h2, logits):
        h2_ref[sl, :] = x
        comb_ref[sl, :] = _route(lg)


MERGE_GROUP_ROWS = 256


def _merge(h, gates, oa, cb, oc, mem_k, mem_v, lp, batch, seq_len, alpha):
    t = batch * seq_len
    mem_len = mem_k.shape[1]
    if seq_len >= 2 * MERGE_GROUP_ROWS:
        nb, tm = 1, 2 * MERGE_GROUP_ROWS
        groups = tuple(slice(r * MERGE_GROUP_ROWS, (r + 1) * MERGE_GROUP_ROWS) for r in range(2))
        units = tuple((r, slice(0, MERGE_GROUP_ROWS), 0) for r in range(2))
    else:
        nb = min(batch, max(1, MERGE_GROUP_ROWS // seq_len))
        tm = nb * seq_len
        groups = (slice(0, tm),)
        units = tuple((0, slice(r * seq_len, (r + 1) * seq_len), r) for r in range(nb))
    assert t % tm == 0 and batch % nb == 0 and (seq_len % tm == 0 or tm % seq_len == 0)
    row = lambda n: pl.BlockSpec((tm, n), lambda i: (i, 0))
    mem = pl.BlockSpec((nb, mem_len, D_MODEL), lambda i: ((i * tm // seq_len) // nb, 0, 0))
    half = MLA_HEADS * MLA_V
    return pl.pallas_call(
        functools.partial(_merge_kernel, alpha=alpha, groups=groups, units=units),
        grid=(t // tm,),
        in_specs=[row(D_MODEL), row(N_BRANCH * D_MODEL), row(half), row(CM_CH), row(GDN_HEADS * GDN_DV),
                  _resident((half, D_MODEL)), _resident((CM_CH, D_MODEL)),
                  _resident((GDN_HEADS * GDN_DV, D_MODEL)), _resident((D_MODEL, D_MODEL)),
                  _resident((1, D_MODEL)), _resident((1, D_MODEL)), _resident((D_MODEL, D_MODEL)),
                  mem, mem, _resident((D_MODEL, D_MODEL)), _resident((1, D_MODEL)),
                  _resident((1, D_MODEL)), _resident((D_MODEL, LANE)), _resident((D_MODEL, LANE)),
                  _resident((1, LANE))],
        out_specs=[row(D_MODEL), row(LANE)],
        out_shape=[jax.ShapeDtypeStruct((t, D_MODEL), F32), jax.ShapeDtypeStruct((t, LANE), F32)],
        compiler_params=_params(("parallel",)),
        name="merge",
    )(h, gates, oa, cb, oc, lp["w_pa"], lp["w_pb"], lp["w_pc"], lp["w_out"], lp["ln1_g"], lp["ln1_b"],
      lp["xa_wq"], mem_k, mem_v, lp["xa_wo"], lp["ln2_g"], lp["ln2_b"], lp["w_route_hi"],
      lp["w_route_lo"], lp["b_route"])


def _moe_kernel(x_ref, comb_ref, wg_ref, wu_ref, wd_ref, lg_ref, lb_ref, o_ref, xb_ref, acc_ref, *, alpha):
    e = pl.program_id(1)

    @pl.when(e == 0)
    def _():
        xb_ref[...] = x_ref[...].astype(BF16)
        acc_ref[...] = jnp.zeros_like(acc_ref)

    xb = xb_ref[...]
    lane = lax.broadcasted_iota(jnp.int32, comb_ref.shape, 1)
    cw = jnp.sum(jnp.where(lane == e + MOE_GROUPS, comb_ref[...], 0.0), axis=-1, keepdims=True)
    a = _dot(xb, wg_ref[0])
    b = _dot(xb, wu_ref[0])
    hid = _silu(a) * b * cw
    acc_ref[...] += _dot(hid.astype(BF16), wd_ref[0])

    @pl.when(e == MOE_EXPERTS - 1)
    def _():
        o_ref[...] = _layer_norm(alpha * x_ref[...] + acc_ref[...], lg_ref[...], lb_ref[...])


def _moe(x, comb, lp, alpha):
    t = x.shape[0]
    tm = min(t, 1024)
    assert t % tm == 0
    row = lambda n: pl.BlockSpec((tm, n), lambda i, e: (i, 0))
    return pl.pallas_call(
        functools.partial(_moe_kernel, alpha=alpha),
        grid=(t // tm, MOE_EXPERTS),
        in_specs=[row(D_MODEL), row(LANE),
                  pl.BlockSpec((1, D_MODEL, MOE_FF), lambda i, e: (e, 0, 0)),
                  pl.BlockSpec((1, D_MODEL, MOE_FF), lambda i, e: (e, 0, 0)),
                  pl.BlockSpec((1, MOE_FF, D_MODEL), lambda i, e: (e, 0, 0)),
                  pl.BlockSpec((1, D_MODEL), lambda i, e: (0, 0)),
                  pl.BlockSpec((1, D_MODEL), lambda i, e: (0, 0))],
        out_specs=row(D_MODEL),
        out_shape=jax.ShapeDtypeStruct((t, D_MODEL), F32),
        scratch_shapes=[pltpu.VMEM((tm, D_MODEL), BF16), pltpu.VMEM((tm, D_MODEL), F32)],
        compiler_params=_params(("parallel", "arbitrary")),
        name="moe",
    )(x, comb, lp["moe_wg"], lp["moe_wu"], lp["moe_wd"], lp["ln3_g"], lp["ln3_b"])


def _memkv_kernel(x_ref, wk_ref, wv_ref, k_ref, v_ref):
    xb = x_ref[...].astype(BF16)
    k_ref[...] = _dot(xb, wk_ref[...])
    v_ref[...] = _dot(xb, wv_ref[...])


def _memkv(mem, lp):
    t = mem.shape[0]
    full = lambda r, c: pl.BlockSpec((r, c), lambda i: (0, 0))
    return pl.pallas_call(
        _memkv_kernel,
        grid=(1,),
        in_specs=[full(t, D_MODEL), full(D_MODEL, D_MODEL), full(D_MODEL, D_MODEL)],
        out_specs=[full(t, D_MODEL), full(t, D_MODEL)],
        out_shape=[jax.ShapeDtypeStruct((t, D_MODEL), F32)] * 2,
        compiler_params=_params(("arbitrary",)),
        name="memkv",
    )(mem, lp["xa_wk"], lp["xa_wv"])


def _cast_kernel(x_ref, o_ref):
    o_ref[...] = x_ref[...].astype(o_ref.dtype)


def _layer_to_bf16(w, l):
    _, e, r, c = w.shape
    per_step = 4 if e % 4 == 0 else 1
    return pl.pallas_call(
        _cast_kernel,
        grid=(e // per_step,),
        in_specs=[pl.BlockSpec((1, per_step, r, c), lambda i: (l, i, 0, 0))],
        out_specs=pl.BlockSpec((1, per_step, r, c), lambda i: (0, i, 0, 0)),
        out_shape=jax.ShapeDtypeStruct((1, e, r, c), BF16),
        compiler_params=_params(("parallel",)),
        name="cast",
    )(w)[0]


def _rot_half_cols(w):
    half = w.shape[-1] // 2
    return jnp.concatenate([-w[..., half:], w[..., :half]], axis=-1)


def _pack_layer(l, p):
    d = D_MODEL
    cols, start = [], 0
    for s in IN_SIZES:
        cols.append(p["w_in"][l][:, start:start + s])
        start += s
    w_gate, w_cq, w_ckv, w_kr, w_glu, w_qkv, w_gz, w_ga, w_gb = cols
    zpad = lambda n: jnp.zeros((d, n), F32)
    w_in = jnp.concatenate([
        w_gate, w_cq, w_ckv,
        w_kr, zpad(LANE - MLA_ROPE),
        _rot_half_cols(w_kr), zpad(LANE - MLA_ROPE),
        w_glu, w_qkv, w_gz,
        w_ga, w_gb, zpad(LANE - 2 * GDN_HEADS)], axis=1).astype(BF16)
    assert w_in.shape[1] == IN_PACKED

    wuq = p["mla_w_uq"][l].reshape(MLA_Q_LORA, MLA_HEADS, MLA_NOPE + MLA_ROPE)
    nope, rope = wuq[..., :MLA_NOPE], wuq[..., MLA_NOPE:]
    pad_q = HEAD_SLOT - MLA_NOPE - MLA_ROPE
    zq = lambda n: jnp.zeros((MLA_Q_LORA, MLA_HEADS, n), F32)
    wq = jnp.concatenate([nope, rope, zq(pad_q)], -1).reshape(MLA_Q_LORA, MLA_HEADS * HEAD_SLOT)
    wqr = jnp.concatenate([zq(MLA_NOPE), _rot_half_cols(rope), zq(pad_q)], -1).reshape(
        MLA_Q_LORA, MLA_HEADS * HEAD_SLOT)

    wukv = p["mla_w_ukv"][l].reshape(MLA_KV_LORA, MLA_HEADS, MLA_NOPE + MLA_V)
    wk = jnp.concatenate([wukv[..., :MLA_NOPE],
                          jnp.zeros((MLA_KV_LORA, MLA_HEADS, HEAD_SLOT - MLA_NOPE), F32)], -1)
    wk = wk.reshape(MLA_KV_LORA, MLA_HEADS * HEAD_SLOT)
    wv_t = jnp.concatenate([wukv[..., MLA_NOPE:],
                            jnp.zeros((MLA_KV_LORA, MLA_HEADS, VT_ROWS - MLA_V), F32)], -1)
    wv_t = wv_t.reshape(MLA_KV_LORA, MLA_HEADS * VT_ROWS).T
    place = jnp.concatenate([jnp.zeros((MLA_ROPE, ROPE_LANE0), F32), jnp.eye(MLA_ROPE, dtype=F32),
                             jnp.zeros((MLA_ROPE, pad_q), F32)], -1)
    pk = jnp.tile(place, (1, MLA_HEADS))
    w_abs = jnp.concatenate([jnp.transpose(wukv[..., :MLA_NOPE], (1, 2, 0)),
                             jnp.zeros((MLA_HEADS, HEAD_SLOT - MLA_NOPE, MLA_KV_LORA), F32)], axis=1)
    head_cols = jnp.eye(MLA_HEADS, dtype=F32)[:, None, :, None]
    wuv_placed = (jnp.transpose(wukv[..., MLA_NOPE:], (1, 0, 2))[:, :, None, :] * head_cols).reshape(
        MLA_HEADS, MLA_KV_LORA, MLA_HEADS * MLA_V)

    half = MLA_ROPE // 2
    inv = ROPE_THETA ** (-jnp.arange(half, dtype=F32) / half)
    inv2 = jnp.concatenate([inv, inv])
    invf = jnp.stack([
        jnp.concatenate([jnp.zeros((ROPE_LANE0,), F32), inv2, jnp.zeros((pad_q,), F32)]),
        jnp.concatenate([inv2, jnp.zeros((LANE - MLA_ROPE,), F32)])])

    lane_pad = lambda v: jnp.concatenate([v, jnp.zeros((LANE - v.shape[0],), F32)])[None, :]
    w_route = jnp.concatenate([p["moe_w_rg"][l], p["moe_w_re"][l],
                               zpad(LANE - MOE_GROUPS - MOE_EXPERTS)], axis=1)
    w_route_hi = w_route.astype(BF16)
    w_route_lo = (w_route - w_route_hi.astype(F32)).astype(BF16)
    b_route = lane_pad(jnp.concatenate([p["moe_b_rg"][l], p["moe_b_re"][l]]))
    r1 = lambda v: v.reshape(1, -1)
    return dict(
        ln_in_g=r1(p["ln_in_g"]), ln_in_b=r1(p["ln_in_b"]),
        w_in=w_in, b_gate=r1(p["b_gate"][l]), q_norm=r1(p["mla_q_norm"][l]),
        wq=wq.astype(BF16), wqr=wqr.astype(BF16), kv_norm=r1(p["mla_kv_norm"][l]), invf=invf,
        wk=wk.astype(BF16), pk=pk.astype(BF16), wv_t=wv_t.astype(BF16),
        w_abs=w_abs.astype(BF16), pk1=place.astype(BF16), wuv_placed=wuv_placed.astype(BF16),
        cm_w_dw=jnp.broadcast_to(p["cm_w_dw"][l][:, None, :], (CM_WIDTH, SUBLANE, CM_CH)),
        cm_b_dw=r1(p["cm_b_dw"][l]), cm_ln_g=r1(p["cm_ln_g"][l]),
        cm_ln_b=r1(p["cm_ln_b"][l]),
        g_w_conv=p["gdn_w_conv"][l], g_a_log=lane_pad(p["gdn_a_log"][l]),
        g_dt_bias=lane_pad(p["gdn_dt_bias"][l]), g_norm=r1(p["gdn_norm"][l]),
        w_pa=p["w_proj_a"][l].astype(BF16), w_pb=p["w_proj_b"][l].astype(BF16),
        w_pc=p["w_proj_c"][l].astype(BF16), w_out=p["w_out"][l].astype(BF16),
        ln1_g=r1(p["ln1_g"][l]), ln1_b=r1(p["ln1_b"][l]),
        xa_wq=p["xa_w_q"][l].astype(BF16), xa_wk=p["xa_w_k"][l].astype(BF16),
        xa_wv=p["xa_w_v"][l].astype(BF16), xa_wo=p["xa_w_o"][l].astype(BF16),
        ln2_g=r1(p["ln2_g"][l]), ln2_b=r1(p["ln2_b"][l]),
        w_route_hi=w_route_hi, w_route_lo=w_route_lo, b_route=b_route,
        moe_wg=_layer_to_bf16(p["moe_w_gate"], l), moe_wu=_layer_to_bf16(p["moe_w_up"], l),
        moe_wd=_layer_to_bf16(p["moe_w_down"], l),
        ln3_g=r1(p["ln3_g"][l]), ln3_b=r1(p["ln3_b"][l]),
    )


def _front_pad_rows(state, rows):
    b, r, c = state.shape
    return jnp.concatenate([jnp.zeros((b, rows - r, c), state.dtype), state], axis=1)


def _layer(x, lp, *, first, batch, seq_len, past, alpha, cache, cm_state, gc_state, s0, mem_k, mem_v):
    res = _inproj(x, lp, pre_ln=first, seq_len=seq_len, past=past)
    if first:
        h, res = res[0], res[1:]
    else:
        h = x
    gates, qp, ckv, kr, glu, qkv, gz, gab = res
    if cache is None:
        kn, vn_t = _kvup(ckv, kr, lp)
        oa = _attn_prompt(qp, kn, vn_t, batch, seq_len)
    else:
        ckv_cache, kr_cache, layer = cache
        oa = _attn_sample(qp, ckv, kr, ckv_cache, kr_cache, layer, lp, batch, seq_len)
    cb = _conformer(glu, _front_pad_rows(cm_state, CM_HALO), lp, batch, seq_len)
    oc, s_new = _gdn(qkv, gab, gz, _front_pad_rows(gc_state, GDN_HALO), s0, lp, batch, seq_len)
    h2, comb = _merge(h, gates, oa, cb, oc, mem_k, mem_v, lp, batch, seq_len, alpha)
    h3 = _moe(h2, comb, lp, alpha)
    assert seq_len >= CM_WIDTH - 1
    states = (ckv.reshape(batch, seq_len, MLA_KV_LORA), kr.reshape(batch, seq_len, MLA_ROPE),
              glu.reshape(batch, seq_len, CM_CH)[:, seq_len - (CM_WIDTH - 1):],
              qkv.reshape(batch, seq_len, GDN_QKV)[:, seq_len - (GDN_CONV - 1):], s_new)
    return h3, states


def kernel(x_prompt, x_sample, mem_prompt, cache_mla_ckv, cache_mla_krope, state_cm_conv, state_gdn_conv, state_gdn, cache_mem_k, cache_mem_v, ln_in_g, ln_in_b, w_in, b_gate, mla_q_norm, mla_w_uq, mla_kv_norm, mla_w_ukv, w_proj_a, cm_w_dw, cm_b_dw, cm_ln_g, cm_ln_b, w_proj_b, gdn_w_conv, gdn_a_log, gdn_dt_bias, gdn_norm, w_proj_c, w_out, ln1_g, ln1_b, xa_w_q, xa_w_k, xa_w_v, xa_w_o, ln2_g, ln2_b, moe_w_rg, moe_b_rg, moe_w_re, moe_b_re, moe_w_gate, moe_w_up, moe_w_down, ln3_g, ln3_b):
    p = dict(ln_in_g=ln_in_g, ln_in_b=ln_in_b, w_in=w_in, b_gate=b_gate, mla_q_norm=mla_q_norm,
             mla_w_uq=mla_w_uq, mla_kv_norm=mla_kv_norm, mla_w_ukv=mla_w_ukv, w_proj_a=w_proj_a,
             cm_w_dw=cm_w_dw, cm_b_dw=cm_b_dw, cm_ln_g=cm_ln_g, cm_ln_b=cm_ln_b, w_proj_b=w_proj_b,
             gdn_w_conv=gdn_w_conv, gdn_a_log=gdn_a_log, gdn_dt_bias=gdn_dt_bias, gdn_norm=gdn_norm,
             w_proj_c=w_proj_c, w_out=w_out, ln1_g=ln1_g, ln1_b=ln1_b, xa_w_q=xa_w_q, xa_w_k=xa_w_k,
             xa_w_v=xa_w_v, xa_w_o=xa_w_o, ln2_g=ln2_g, ln2_b=ln2_b, moe_w_rg=moe_w_rg,
             moe_b_rg=moe_b_rg, moe_w_re=moe_w_re, moe_b_re=moe_b_re, moe_w_gate=moe_w_gate,
             moe_w_up=moe_w_up, moe_w_down=moe_w_down, ln3_g=ln3_g, ln3_b=ln3_b)
    depth = w_in.shape[0]
    alpha = (2 * depth) ** 0.25
    b_p, l_p, d = x_prompt.shape
    b_s, l_s, _ = x_sample.shape
    past = cache_mla_ckv.shape[2]
    mem_len = mem_prompt.shape[1]
    hp = x_prompt.reshape(b_p * l_p, d)
    hs = x_sample.reshape(b_s * l_s, d)
    cm0 = jnp.zeros((b_p, CM_WIDTH - 1, CM_CH), F32)
    gc0 = jnp.zeros((b_p, GDN_CONV - 1, GDN_QKV), F32)
    s0 = jnp.zeros((b_p, GDN_HEADS, GDN_DK, GDN_DV), F32)
    outs_p = [[] for _ in range(7)]
    outs_s = [[] for _ in range(5)]
    for l in range(depth):
        lp = _pack_layer(l, p)
        mk, mv = _memkv(mem_prompt.reshape(b_p * mem_len, d), lp)
        mk = mk.reshape(b_p, mem_len, d)
        mv = mv.reshape(b_p, mem_len, d)
        hp, st_p = _layer(hp, lp, first=(l == 0), batch=b_p, seq_len=l_p, past=0, alpha=alpha,
                          cache=None, cm_state=cm0, gc_state=gc0, s0=s0, mem_k=mk, mem_v=mv)
        hs, st_s = _layer(hs, lp, first=(l == 0), batch=b_s, seq_len=l_s, past=past, alpha=alpha,
                          cache=(cache_mla_ckv, cache_mla_krope, l), cm_state=state_cm_conv[l],
                          gc_state=state_gdn_conv[l], s0=state_gdn[l],
                          mem_k=cache_mem_k[l].reshape(b_s, mem_len, d),
                          mem_v=cache_mem_v[l].reshape(b_s, mem_len, d))
        mk4 = mk.reshape(b_p, mem_len, XA_HEADS, XA_DIM)
        mv4 = mv.reshape(b_p, mem_len, XA_HEADS, XA_DIM)
        for lst, arr in zip(outs_p, st_p + (mk4, mv4)):
            lst.append(arr)
        for lst, arr in zip(outs_s, st_s):
            lst.append(arr)
    ckv_p, kr_p, cm_p, gc_p, gdn_p, mk_p, mv_p = [jnp.stack(a) for a in outs_p]
    ckv_s, kr_s, cm_s, gc_s, gdn_s = [jnp.stack(a) for a in outs_s]
    return (hp.reshape(b_p, l_p, d), hs.reshape(b_s, l_s, d), ckv_p, kr_p, ckv_s, kr_s,
            cm_p, cm_s, gc_p, gc_s, gdn_p, gdn_s, mk_p, mv_p)
```

```python
import functools

import jax
import jax.numpy as jnp
from jax import lax
from jax.experimental import pallas as pl
from jax.experimental.pallas import tpu as pltpu

F32 = jnp.float32
BF16 = jnp.bfloat16

D_MODEL = 1024
CHUNK = 64
MLA_HEADS = 8
MLA_NOPE = 64
MLA_ROPE = 32
MLA_V = 64
MLA_Q_LORA = 384
MLA_KV_LORA = 256
MLA_SCALE = (MLA_NOPE + MLA_ROPE) ** -0.5
LOG2E = 1.4426950408889634
ROPE_THETA = 10000.0
CM_CH = 512
CM_WIDTH = 31
GDN_HEADS = 4
GDN_DK = 128
GDN_DV = 128
GDN_CONV = 4
GDN_QKV = GDN_HEADS * (2 * GDN_DK + GDN_DV)
XA_HEADS = 4
XA_DIM = D_MODEL // XA_HEADS
MOE_GROUPS = 4
MOE_PER_GROUP = 4
MOE_EXPERTS = MOE_GROUPS * MOE_PER_GROUP
MOE_FF = 256
N_BRANCH = 3
LN_EPS = 1e-5
RMS_EPS = 1e-6
IN_SIZES = (N_BRANCH * D_MODEL, MLA_Q_LORA, MLA_KV_LORA, MLA_ROPE, 2 * CM_CH,
            GDN_QKV, GDN_HEADS * GDN_DV, GDN_HEADS, GDN_HEADS)

LANE = 128
BF16_ROWS = 16
VT_ROWS = MLA_V + BF16_ROWS
SUBLANE = 8
HEAD_SLOT = 128
ROPE_LANE0 = MLA_NOPE
OFF_GATE = 0
OFF_CQ = OFF_GATE + N_BRANCH * D_MODEL
OFF_CKV = OFF_CQ + MLA_Q_LORA
OFF_KR = OFF_CKV + MLA_KV_LORA
OFF_KRR = OFF_KR + LANE
OFF_GLU = OFF_KRR + LANE
OFF_QKV = OFF_GLU + 2 * CM_CH
OFF_GZ = OFF_QKV + GDN_QKV
OFF_GAB = OFF_GZ + GDN_HEADS * GDN_DV
IN_PACKED = OFF_GAB + LANE
CM_HALO = 32
GDN_HALO = 8
VMEM_LIMIT = 56 * 1024 * 1024


def _dot(a, b):
    return jnp.dot(a, b, preferred_element_type=F32)


def _dot_nt(a, b):
    return lax.dot_general(a, b, (((1,), (1,)), ((), ())), preferred_element_type=F32)


def _layer_norm(x, g, b):
    xc = x - jnp.mean(x, axis=-1, keepdims=True)
    var = jnp.mean(xc * xc, axis=-1, keepdims=True)
    return xc * lax.rsqrt(var + LN_EPS) * g + b


def _rms_norm(x, g):
    return x * lax.rsqrt(jnp.mean(x * x, axis=-1, keepdims=True) + RMS_EPS) * g


def _sigmoid(x):
    return 1.0 / (1.0 + jnp.exp(-x))


def _silu(x):
    return x * _sigmoid(x)


def _resident(shape):
    nd = len(shape)
    return pl.BlockSpec(shape, lambda *_: (0,) * nd, pipeline_mode=pl.Buffered(1))


def _params(sem, vmem=VMEM_LIMIT):
    return pltpu.CompilerParams(dimension_semantics=sem, vmem_limit_bytes=vmem)


def _inproj_kernel(x_ref, lng_ref, lnb_ref, w_ref, bg_ref, qn_ref, wq_ref, wqr_ref, kvn_ref,
                   invf_ref, *outs, pre_ln, seq_len, past, tm):
    if pre_ln:
        h_ref, outs = outs[0], outs[1:]
    gates_ref, qp_ref, ckv_ref, kr_ref, glu_ref, qkv_ref, gz_ref, gab_ref = outs
    x = x_ref[...]
    if pre_ln:
        x = _layer_norm(x, lng_ref[...], lnb_ref[...])
        h_ref[...] = x
    xb = x.astype(BF16)

    def grp(off, n):
        return _dot(xb, w_ref[:, off:off + n])

    for j in range(N_BRANCH):
        sl = slice(j * D_MODEL, (j + 1) * D_MODEL)
        gates_ref[:, sl] = _sigmoid(grp(OFF_GATE + j * D_MODEL, D_MODEL) + bg_ref[:, sl])

    row = pl.program_id(0) * tm + lax.broadcasted_iota(jnp.int32, (tm, 1), 0)
    pos = (row % seq_len + past).astype(F32)
    ang_q = pos * invf_ref[0:1, :]
    ang_k = pos * invf_ref[1:2, :]
    cos_q, sin_q = jnp.cos(ang_q) * (MLA_SCALE * LOG2E), jnp.sin(ang_q) * (MLA_SCALE * LOG2E)
    cos_k, sin_k = jnp.cos(ang_k), jnp.sin(ang_k)

    qn = _rms_norm(grp(OFF_CQ, MLA_Q_LORA), qn_ref[...]).astype(BF16)
    for h in range(MLA_HEADS):
        sl = slice(h * HEAD_SLOT, (h + 1) * HEAD_SLOT)
        q1 = _dot(qn, wq_ref[:, sl])
        q2 = _dot(qn, wqr_ref[:, sl])
        qp_ref[:, sl] = (q1 * cos_q + q2 * sin_q).astype(BF16)

    ckv_ref[...] = _rms_norm(grp(OFF_CKV, MLA_KV_LORA), kvn_ref[...])
    kr = grp(OFF_KR, LANE) * cos_k + grp(OFF_KRR, LANE) * sin_k
    kr_ref[...] = kr[:, :MLA_ROPE]

    glu_ref[...] = grp(OFF_GLU, CM_CH) * _sigmoid(grp(OFF_GLU + CM_CH, CM_CH))
    for j in range(GDN_QKV // 512):
        qkv_ref[:, j * 512:(j + 1) * 512] = grp(OFF_QKV + j * 512, 512)
    gz_ref[...] = grp(OFF_GZ, GDN_HEADS * GDN_DV)
    gab_ref[...] = grp(OFF_GAB, LANE)


def _inproj(x, lp, *, pre_ln, seq_len, past, tm=256):
    t = x.shape[0]
    tm = min(tm, t)
    assert t % tm == 0
    row = lambda n: pl.BlockSpec((tm, n), lambda i: (i, 0))
    out_shape = [
        jax.ShapeDtypeStruct((t, N_BRANCH * D_MODEL), F32),
        jax.ShapeDtypeStruct((t, MLA_HEADS * HEAD_SLOT), BF16),
        jax.ShapeDtypeStruct((t, MLA_KV_LORA), F32),
        jax.ShapeDtypeStruct((t, MLA_ROPE), F32),
        jax.ShapeDtypeStruct((t, CM_CH), F32),
        jax.ShapeDtypeStruct((t, GDN_QKV), F32),
        jax.ShapeDtypeStruct((t, GDN_HEADS * GDN_DV), F32),
        jax.ShapeDtypeStruct((t, LANE), F32),
    ]
    out_specs = [row(s.shape[1]) for s in out_shape]
    if pre_ln:
        out_shape = [jax.ShapeDtypeStruct((t, D_MODEL), F32)] + out_shape
        out_specs = [row(D_MODEL)] + out_specs
    return pl.pallas_call(
        functools.partial(_inproj_kernel, pre_ln=pre_ln, seq_len=seq_len, past=past, tm=tm),
        grid=(t // tm,),
        in_specs=[row(D_MODEL), _resident((1, D_MODEL)), _resident((1, D_MODEL)),
                  _resident((D_MODEL, IN_PACKED)), _resident((1, N_BRANCH * D_MODEL)),
                  _resident((1, MLA_Q_LORA)), _resident((MLA_Q_LORA, MLA_HEADS * HEAD_SLOT)),
                  _resident((MLA_Q_LORA, MLA_HEADS * HEAD_SLOT)), _resident((1, MLA_KV_LORA)),
                  _resident((2, LANE))],
        out_specs=out_specs,
        out_shape=out_shape,
        compiler_params=_params(("parallel",)),
        name="inproj",
    )(x, lp["ln_in_g"], lp["ln_in_b"], lp["w_in"], lp["b_gate"], lp["q_norm"], lp["wq"], lp["wqr"],
      lp["kv_norm"], lp["invf"])


def _kvup_kernel(ckv_ref, kr_ref, wk_ref, pk_ref, wv_ref, k_ref, vt_ref):
    c = ckv_ref[...].astype(BF16)
    r = kr_ref[...].astype(BF16)
    k_ref[...] = (_dot(c, wk_ref[...]) + _dot(r, pk_ref[...])).astype(BF16)
    vt = _dot_nt(wv_ref[...], c)
    row = lax.broadcasted_iota(jnp.int32, vt.shape, 0)
    vt_ref[...] = jnp.where(row % VT_ROWS == MLA_V, 1.0, vt).astype(BF16)


def _kvup(ckv, kr, lp, tm=512):
    t = ckv.shape[0]
    tm = min(tm, t)
    assert t % tm == 0
    row = lambda n: pl.BlockSpec((tm, n), lambda i: (i, 0))
    vt_rows = MLA_HEADS * VT_ROWS
    return pl.pallas_call(
        _kvup_kernel,
        grid=(t // tm,),
        in_specs=[row(MLA_KV_LORA), row(MLA_ROPE), _resident((MLA_KV_LORA, MLA_HEADS * HEAD_SLOT)),
                  _resident((MLA_ROPE, MLA_HEADS * HEAD_SLOT)), _resident((vt_rows, MLA_KV_LORA))],
        out_specs=[row(MLA_HEADS * HEAD_SLOT), pl.BlockSpec((vt_rows, tm), lambda i: (0, i))],
        out_shape=[jax.ShapeDtypeStruct((t, MLA_HEADS * HEAD_SLOT), BF16),
                   jax.ShapeDtypeStruct((vt_rows, t), BF16)],
        compiler_params=_params(("parallel",)),
        name="kvup",
    )(ckv, kr, lp["wk"], lp["pk"], lp["wv_t"])


def _attn_prompt_kernel(q_ref, k_ref, vt_ref, o_ref, sa_ref, sb_ref, *, tq, tk):
    i = pl.program_id(2)
    kc = lax.broadcasted_iota(jnp.int32, (tk, tq), 0) // CHUNK
    qc = lax.broadcasted_iota(jnp.int32, (tk, tq), 1) // CHUNK
    qs = [q_ref[:, j * HEAD_SLOT:(j + 1) * HEAD_SLOT] for j in range(2)]

    def stage_scores(ref, kb):
        start = pl.multiple_of(kb * tk, tk)
        for j in range(2):
            ref[j] = _dot_nt(k_ref[pl.ds(start, tk), j * HEAD_SLOT:(j + 1) * HEAD_SLOT], qs[j])

    def update(kb, ref, c, visible=None):
        start = pl.multiple_of(kb * tk, tk)
        out = []
        for j in range(2):
            m, acc = c[2 * j:2 * j + 2]
            s = ref[j] if visible is None else jnp.where(visible, ref[j], -jnp.inf)
            m_new = jnp.maximum(m, jnp.max(s, axis=0, keepdims=True))
            alpha = jnp.exp2(m - m_new)
            p = jnp.exp2(s - m_new)
            vt = vt_ref[j * VT_ROWS:(j + 1) * VT_ROWS, pl.ds(start, tk)]
            out += [m_new, alpha * acc + _dot(vt, p.astype(BF16))]
        return tuple(out)

    stage_scores(sa_ref, 0)
    c = []
    for j in range(2):
        c += [jnp.full((1, tq), -jnp.inf, F32), jnp.zeros((VT_ROWS, tq), F32)]

    def pair(t, cc):
        stage_scores(sb_ref, 2 * t + 1)
        cc = update(2 * t, sa_ref, cc)
        stage_scores(sa_ref, 2 * t + 2)
        return update(2 * t + 1, sb_ref, cc)

    def quad(t, cc):
        return pair(2 * t + 1, pair(2 * t, cc))

    c = lax.fori_loop(0, i // 4, lambda t, cc: quad(2 * t + 1, quad(2 * t, cc)), tuple(c))
    c = lax.fori_loop(0, (i // 2) % 2, lambda _, cc: quad(i // 2 - 1, cc), c)
    c = lax.fori_loop(0, i % 2, lambda _, cc: pair(i - 1, cc), c)
    stage_scores(sb_ref, 2 * i + 1)
    c = update(2 * i, sa_ref, c, visible=kc <= qc)
    c = update(2 * i + 1, sb_ref, c, visible=kc + tk // CHUNK <= qc)
    o_t = jnp.concatenate([c[2 * j + 1][:MLA_V] / c[2 * j + 1][MLA_V:MLA_V + 1] for j in range(2)], axis=0)
    o_ref[...] = o_t.T.astype(BF16)


def _attn_prompt(qp, kp, vt, batch, seq_len, tk=256):
    tq = 2 * tk
    assert seq_len % tq == 0 and tk % CHUNK == 0
    nq = seq_len // tq
    t = batch * seq_len
    return pl.pallas_call(
        functools.partial(_attn_prompt_kernel, tq=tq, tk=tk),
        grid=(batch, MLA_HEADS // 2, nq),
        in_specs=[pl.BlockSpec((tq, 2 * HEAD_SLOT), lambda b, hp, i: (b * nq + i, hp)),
                  pl.BlockSpec((seq_len, 2 * HEAD_SLOT), lambda b, hp, i: (b, hp)),
                  pl.BlockSpec((2 * VT_ROWS, seq_len), lambda b, hp, i: (hp, b))],
        out_specs=pl.BlockSpec((tq, 2 * MLA_V), lambda b, hp, i: (b * nq + i, hp)),
        out_shape=jax.ShapeDtypeStruct((t, MLA_HEADS * MLA_V), BF16),
        scratch_shapes=[pltpu.VMEM((2, tk, tq), F32), pltpu.VMEM((2, tk, tq), F32)],
        compiler_params=_params(("parallel", "parallel", "arbitrary")),
        name="attn_prompt",
    )(qp, kp, vt)


def _attn_sample_kernel(q_ref, cp_ref, rp_ref, cn_ref, rn_ref, wabs_ref, pk_ref, wuv_ref, o_ref):
    heads = range(MLA_HEADS)
    n_q = q_ref.shape[0]
    q_heads = [q_ref[:, h * HEAD_SLOT:(h + 1) * HEAD_SLOT] for h in heads]
    q_all = jnp.concatenate(q_heads, axis=0)
    q_lat = jnp.concatenate([_dot(q_heads[h], wabs_ref[h]).astype(BF16) for h in heads], axis=0)

    def latent_keys(c, r):
        return c.astype(BF16), _dot(r.astype(BF16), pk_ref[...]).astype(BF16)

    cp, rp = latent_keys(cp_ref[0, 0], rp_ref[0, 0])
    cn, rn = latent_keys(cn_ref[...], rn_ref[...])
    sp = _dot_nt(q_lat, cp) + _dot_nt(q_all, rp)
    sn = _dot_nt(q_lat, cn) + _dot_nt(q_all, rn)
    m = jnp.maximum(jnp.max(sp, axis=-1, keepdims=True), jnp.max(sn, axis=-1, keepdims=True))
    pp, pn = jnp.exp2(sp - m), jnp.exp2(sn - m)
    l = jnp.sum(pp, axis=-1, keepdims=True) + jnp.sum(pn, axis=-1, keepdims=True)
    o_lat = ((_dot(pp.astype(BF16), cp) + _dot(pn.astype(BF16), cn)) / l).astype(BF16)
    o = _dot(o_lat[0:n_q], wuv_ref[0])
    for h in range(1, MLA_HEADS):
        o = o + _dot(o_lat[h * n_q:(h + 1) * n_q], wuv_ref[h])
    o_ref[...] = o.astype(BF16)


def _attn_sample(qp, ckv_new, kr_new, ckv_cache, kr_cache, layer, lp, batch, seq_len):
    past = ckv_cache.shape[2]
    new = lambda n: pl.BlockSpec((seq_len, n), lambda b: (b, 0))
    old = lambda n: pl.BlockSpec((1, 1, past, n), lambda b: (layer, b, 0, 0))
    return pl.pallas_call(
        _attn_sample_kernel,
        grid=(batch,),
        in_specs=[new(MLA_HEADS * HEAD_SLOT), old(MLA_KV_LORA), old(MLA_ROPE), new(MLA_KV_LORA),
                  new(MLA_ROPE), _resident((MLA_HEADS, HEAD_SLOT, MLA_KV_LORA)),
                  _resident((MLA_ROPE, HEAD_SLOT)),
                  _resident((MLA_HEADS, MLA_KV_LORA, MLA_HEADS * MLA_V))],
        out_specs=new(MLA_HEADS * MLA_V),
        out_shape=jax.ShapeDtypeStruct((batch * seq_len, MLA_HEADS * MLA_V), BF16),
        compiler_params=_params(("parallel",)),
        name="attn_sample",
    )(qp, ckv_cache, kr_cache, ckv_new, kr_new, lp["w_abs"], lp["pk1"], lp["wuv_placed"])


def _conformer_kernel(x_ref, st_ref, w_ref, b_ref, g_ref, beta_ref, o_ref, xbuf, xs, *, tl, sub):
    @pl.when(pl.program_id(1) == 0)
    def _():
        xbuf[0:CM_HALO, :] = st_ref[0]

    xbuf[CM_HALO:CM_HALO + tl, :] = x_ref[...]
    first = CM_HALO - (CM_WIDTH - 1)
    span = xs.shape[1]
    for b in range(1, SUBLANE):
        xs[b - 1] = xbuf[b:b + span, :]
    for r in range(tl // sub):
        acc = jnp.zeros((sub // SUBLANE, SUBLANE, CM_CH), F32)
        for j in range(CM_WIDTH):
            b = (first + j) % SUBLANE
            lo = r * sub + first + j - b
            win = xbuf[lo:lo + sub, :] if b == 0 else xs[b - 1, lo:lo + sub, :]
            acc = acc + w_ref[j][None] * win.reshape(sub // SUBLANE, SUBLANE, CM_CH)
        acc = acc.reshape(sub, CM_CH) + b_ref[...]
        y = _layer_norm(acc, g_ref[...], beta_ref[...])
        o_ref[r * sub:(r + 1) * sub, :] = _silu(y).astype(BF16)
    xbuf[0:CM_HALO, :] = xbuf[tl:tl + CM_HALO, :]


def _conformer(glu, state, lp, batch, seq_len):
    tl = min(seq_len, 512)
    assert seq_len % tl == 0 and tl >= CM_HALO
    nl = seq_len // tl
    span = tl + (CM_HALO - SUBLANE)
    return pl.pallas_call(
        functools.partial(_conformer_kernel, tl=tl, sub=32),
        grid=(batch, nl),
        in_specs=[pl.BlockSpec((tl, CM_CH), lambda b, l: (b * nl + l, 0)),
                  pl.BlockSpec((1, CM_HALO, CM_CH), lambda b, l: (b, 0, 0)),
                  _resident((CM_WIDTH, SUBLANE, CM_CH)), _resident((1, CM_CH)), _resident((1, CM_CH)),
                  _resident((1, CM_CH))],
        out_specs=pl.BlockSpec((tl, CM_CH), lambda b, l: (b * nl + l, 0)),
        out_shape=jax.ShapeDtypeStruct((batch * seq_len, CM_CH), BF16),
        scratch_shapes=[pltpu.VMEM((CM_HALO + tl, CM_CH), F32),
                        pltpu.VMEM((SUBLANE - 1, span, CM_CH), F32)],
        compiler_params=_params(("parallel", "arbitrary")),
        name="conformer",
    )(glu, state, lp["cm_w_dw"], lp["cm_b_dw"], lp["cm_ln_g"], lp["cm_ln_b"])


def _split_bf16(a):
    hi = a.astype(BF16)
    return hi, (a - hi.astype(F32)).astype(BF16)


def _dot3(a, b):
    (ah, al), (bh, bl) = a, b
    return _dot(ah, bh) + (_dot(ah, bl) + _dot(al, bh))


def _gdn_kernel(qkv_ref, gab_ref, gz_ref, cst_ref, s0_ref, wc_ref, alog_ref, dtb_ref, gn_ref,
                o_ref, s_ref, xbuf, xc, *, rows):
    @pl.when(pl.program_id(1) == 0)
    def _():
        xbuf[0:GDN_HALO, :] = cst_ref[0]
        s_ref[...] = s0_ref[...]

    xbuf[GDN_HALO:GDN_HALO + rows, :] = qkv_ref[...]
    for c in range(GDN_QKV // LANE):
        sl = slice(c * LANE, (c + 1) * LANE)
        acc = wc_ref[GDN_CONV - 1:GDN_CONV, sl] * xbuf[GDN_HALO:GDN_HALO + rows, sl]
        for j in range(GDN_CONV - 1):
            lo = GDN_HALO - (GDN_CONV - 1) + j
            acc = acc + wc_ref[j:j + 1, sl] * xbuf[lo:lo + rows, sl]
        xc[:, sl] = _silu(acc)
    xbuf[0:GDN_HALO, :] = xbuf[rows:rows + GDN_HALO, :]

    gab = gab_ref[...]
    z = gab + dtb_ref[...]
    softplus = jnp.maximum(z, 0.0) + jnp.log(1.0 + jnp.exp(-jnp.abs(z)))
    g = -jnp.exp(alog_ref[...]) * softplus
    beta_all = _sigmoid(gab)
    ri = lax.broadcasted_iota(jnp.int32, (rows, rows), 0)
    ci = lax.broadcasted_iota(jnp.int32, (rows, rows), 1)
    tri = jnp.where((ri >= ci) & (ri // CHUNK == ci // CHUNK), 1.0, 0.0).astype(BF16)
    g1 = g.astype(BF16)
    r1 = g - g1.astype(F32)
    g2 = r1.astype(BF16)
    g3 = (r1 - g2.astype(F32)).astype(BF16)
    gcs = _dot(tri, g1) + _dot(tri, g2) + _dot(tri, g3)
    gcs_t = gcs.T

    i64 = lax.broadcasted_iota(jnp.int32, (CHUNK, CHUNK), 0)
    j64 = lax.broadcasted_iota(jnp.int32, (CHUNK, CHUNK), 1)
    causal = i64 >= j64
    strict = i64 > j64
    eye = jnp.where(i64 == j64, 1.0, 0.0).astype(F32)
    dq, dk = GDN_HEADS * GDN_DK, GDN_HEADS * GDN_DK
    n_chunks = rows // CHUNK
    units = [(c, h) for c in range(n_chunks) for h in range(GDN_HEADS)]

    U = {}
    for c, h in units:
        rs = slice(c * CHUNK, (c + 1) * CHUNK)
        q = xc[rs, h * GDN_DK:(h + 1) * GDN_DK]
        k = xc[rs, dq + h * GDN_DK:dq + (h + 1) * GDN_DK]
        v = xc[rs, dq + dk + h * GDN_DV:dq + dk + (h + 1) * GDN_DV]
        q = q * lax.rsqrt(jnp.sum(q * q, axis=-1, keepdims=True) + 1e-6) * (GDN_DK ** -0.5)
        k = k * lax.rsqrt(jnp.sum(k * k, axis=-1, keepdims=True) + 1e-6)
        gcol = gcs[rs, h:h + 1]
        grow = gcs_t[h:h + 1, rs]
        beta = beta_all[rs, GDN_HEADS + h:GDN_HEADS + h + 1]
        decay = jnp.exp(jnp.where(causal, gcol - grow, -jnp.inf))
        kbeta = k * beta
        egc = jnp.exp(gcol)
        glast = gcol[CHUNK - 1:CHUNK, :]
        U[c, h] = dict(
            a=jnp.where(strict, _dot_nt(kbeta, k) * decay, 0.0),
            qk=(_dot_nt(q, k) * decay).astype(BF16),
            rhs=jnp.concatenate([v * beta, kbeta * egc], axis=1),
            qe=(q * egc).astype(BF16),
            kd=(k * jnp.exp(glast - gcol)).astype(BF16),
            sdecay=jnp.exp(glast))

    x = {u: eye - U[u]["a"] for u in units}
    p = {u: _split_bf16(U[u]["a"]) for u in units}
    k2 = 2
    while k2 < CHUNK:
        p2 = {u: _dot3(p[u], p[u]) for u in units}
        p = {u: _split_bf16(p2[u]) for u in units}
        x = {u: x[u] + _dot3(_split_bf16(x[u]), p[u]) for u in units}
        k2 *= 2
    sol = {u: _dot3(_split_bf16(x[u]), _split_bf16(U[u]["rhs"])) for u in units}

    for c in range(n_chunks):
        rs = slice(c * CHUNK, (c + 1) * CHUNK)
        s_old = [s_ref[0, h] for h in range(GDN_HEADS)]
        s_bf = [s.astype(BF16) for s in s_old]
        ws = [_dot(sol[c, h][:, GDN_DV:].astype(BF16), s_bf[h]) for h in range(GDN_HEADS)]
        qs = [_dot(U[c, h]["qe"], s_bf[h]) for h in range(GDN_HEADS)]
        v_new = [(sol[c, h][:, :GDN_DV] - ws[h]).astype(BF16) for h in range(GDN_HEADS)]
        for h in range(GDN_HEADS):
            s_ref[0, h] = s_old[h] * U[c, h]["sdecay"] + lax.dot_general(
                U[c, h]["kd"], v_new[h], (((0,), (0,)), ((), ())), preferred_element_type=F32)
        for h in range(GDN_HEADS):
            o = qs[h] + _dot(U[c, h]["qk"], v_new[h])
            gate = _silu(gz_ref[rs, h * GDN_DV:(h + 1) * GDN_DV])
            o_ref[rs, h * GDN_DV:(h + 1) * GDN_DV] = (_rms_norm(o, gn_ref[...]) * gate).astype(BF16)


def _gdn(qkv, gab, gz, conv_state, s0, lp, batch, seq_len):
    rows = min(seq_len, 256)
    assert seq_len % rows == 0 and rows % CHUNK == 0
    nl = seq_len // rows
    tile = lambda n: pl.BlockSpec((rows, n), lambda b, l: (b * nl + l, 0))
    return pl.pallas_call(
        functools.partial(_gdn_kernel, rows=rows),
        grid=(batch, nl),
        in_specs=[tile(GDN_QKV), tile(LANE), tile(GDN_HEADS * GDN_DV),
                  pl.BlockSpec((1, GDN_HALO, GDN_QKV), lambda b, l: (b, 0, 0)),
                  pl.BlockSpec((1, GDN_HEADS, GDN_DK, GDN_DV), lambda b, l: (b, 0, 0, 0)),
                  _resident((GDN_CONV, GDN_QKV)), _resident((1, LANE)), _resident((1, LANE)),
                  _resident((1, GDN_DV))],
        out_specs=[tile(GDN_HEADS * GDN_DV),
                   pl.BlockSpec((1, GDN_HEADS, GDN_DK, GDN_DV), lambda b, l: (b, 0, 0, 0))],
        out_shape=[jax.ShapeDtypeStruct((batch * seq_len, GDN_HEADS * GDN_DV), BF16),
                   jax.ShapeDtypeStruct((batch, GDN_HEADS, GDN_DK, GDN_DV), F32)],
        scratch_shapes=[pltpu.VMEM((GDN_HALO + rows, GDN_QKV), F32),
                        pltpu.VMEM((rows, GDN_QKV), F32)],
        compiler_params=_params(("parallel", "arbitrary")),
        name="gdn",
    )(qkv, gab, gz, conv_state, s0, lp["g_w_conv"], lp["g_a_log"], lp["g_dt_bias"], lp["g_norm"])


def _route(lg):
    lane = lax.broadcasted_iota(jnp.int32, lg.shape, 1)
    is_g = lane < MOE_GROUPS
    gl = jnp.where(is_g, lg, -jnp.inf)
    gmax = jnp.max(gl, axis=-1, keepdims=True)
    gsel = jnp.min(jnp.where(gl == gmax, lane, LANE), axis=-1, keepdims=True)
    gw = 1.0 / jnp.sum(jnp.where(is_g, jnp.exp(lg - gmax), 0.0), axis=-1, keepdims=True)
    e_lo = MOE_GROUPS + gsel * MOE_PER_GROUP
    in_grp = (lane >= e_lo) & (lane < e_lo + MOE_PER_GROUP)
    el = jnp.where(in_grp, lg, -jnp.inf)
    v1 = jnp.max(el, axis=-1, keepdims=True)
    i1 = jnp.min(jnp.where(el == v1, lane, LANE), axis=-1, keepdims=True)
    rest = in_grp & (lane != i1)
    el2 = jnp.where(rest, lg, -jnp.inf)
    v2 = jnp.max(el2, axis=-1, keepdims=True)
    i2 = jnp.min(jnp.where(rest & (el2 == v2), lane, LANE), axis=-1, keepdims=True)
    e2 = jnp.exp(v2 - v1)
    den = 1.0 + e2
    return jnp.where(lane == i1, gw / den, 0.0) + jnp.where(lane == i2, gw * e2 / den, 0.0)


def _merge_kernel(h_ref, g_ref, oa_ref, cb_ref, oc_ref, wpa_ref, wpb_ref, wpc_ref, wout_ref,
                  l1g_ref, l1b_ref, wq_ref, mk_ref, mv_ref, wo_ref, l2g_ref, l2b_ref, wrh_ref, wrl_ref, br_ref,
                  h2_ref, comb_ref, *, alpha, groups, units):
    merged = [(g_ref[sl, 0:D_MODEL] * _dot(oa_ref[sl, :], wpa_ref[...])
               + g_ref[sl, D_MODEL:2 * D_MODEL] * _dot(cb_ref[sl, :], wpb_ref[...])
               + g_ref[sl, 2 * D_MODEL:3 * D_MODEL] * _dot(oc_ref[sl, :], wpc_ref[...])).astype(BF16)
              for sl in groups]
    t = [_dot(m, wout_ref[...]) for m in merged]
    h1 = [_layer_norm(alpha * h_ref[sl, :] + ti, l1g_ref[...], l1b_ref[...]) for sl, ti in zip(groups, t)]
    q = [_dot(x.astype(BF16), wq_ref[...]).astype(BF16) for x in h1]
    att = [[] for _ in units]
    for hh in range(XA_HEADS):
        sl = slice(hh * XA_DIM, (hh + 1) * XA_DIM)
        sc = [_dot_nt(q[gi][rs, sl], mk_ref[mi, :, sl].astype(BF16)) * (XA_DIM ** -0.5)
              for gi, rs, mi in units]
        p = [jnp.exp(si - jnp.max(si, axis=-1, keepdims=True)) for si in sc]
        p = [(pi / jnp.sum(pi, axis=-1, keepdims=True)).astype(BF16) for pi in p]
        for a, pi, (_, _, mi) in zip(att, p, units):
            a.append(_dot(pi, mv_ref[mi, :, sl].astype(BF16)).astype(BF16))
    xo = []
    for gi in range(len(groups)):
        o = jnp.concatenate([jnp.concatenate(a, axis=1) for a, u in zip(att, units) if u[0] == gi], axis=0)
        xo.append(_dot(o, wo_ref[...]))
    h2 = [_layer_norm(alpha * a + b, l2g_ref[...], l2b_ref[...]) for a, b in zip(h1, xo)]
    logits = [_dot3(_split_bf16(x), (wrh_ref[...], wrl_ref[...])) + br_ref[...] for x in h2]
    for sl, x, lg in zip(groups, h2, logits):
        h2_ref[sl, :] = x
        comb_ref[sl, :] = _route(lg)


MERGE_GROUP_ROWS = 256


def _merge(h, gates, oa, cb, oc, mem_k, mem_v, lp, batch, seq_len, alpha):
    t = batch * seq_len
    mem_len = mem_k.shape[1]
    if seq_len >= 2 * MERGE_GROUP_ROWS:
        nb, tm = 1, 2 * MERGE_GROUP_ROWS
        groups = tuple(slice(r * MERGE_GROUP_ROWS, (r + 1) * MERGE_GROUP_ROWS) for r in range(2))
        units = tuple((r, slice(0, MERGE_GROUP_ROWS), 0) for r in range(2))
    else:
        nb = min(batch, max(1, MERGE_GROUP_ROWS // seq_len))
        tm = nb * seq_len
        groups = (slice(0, tm),)
        units = tuple((0, slice(r * seq_len, (r + 1) * seq_len), r) for r in range(nb))
    assert t % tm == 0 and batch % nb == 0 and (seq_len % tm == 0 or tm % seq_len == 0)
    row = lambda n: pl.BlockSpec((tm, n), lambda i: (i, 0))
    mem = pl.BlockSpec((nb, mem_len, D_MODEL), lambda i: ((i * tm // seq_len) // nb, 0, 0))
    half = MLA_HEADS * MLA_V
    return pl.pallas_call(
        functools.partial(_merge_kernel, alpha=alpha, groups=groups, units=units),
        grid=(t // tm,),
        in_specs=[row(D_MODEL), row(N_BRANCH * D_MODEL), row(half), row(CM_CH), row(GDN_HEADS * GDN_DV),
                  _resident((half, D_MODEL)), _resident((CM_CH, D_MODEL)),
                  _resident((GDN_HEADS * GDN_DV, D_MODEL)), _resident((D_MODEL, D_MODEL)),
                  _resident((1, D_MODEL)), _resident((1, D_MODEL)), _resident((D_MODEL, D_MODEL)),
                  mem, mem, _resident((D_MODEL, D_MODEL)), _resident((1, D_MODEL)),
                  _resident((1, D_MODEL)), _resident((D_MODEL, LANE)), _resident((D_MODEL, LANE)),
                  _resident((1, LANE))],
        out_specs=[row(D_MODEL), row(LANE)],
        out_shape=[jax.ShapeDtypeStruct((t, D_MODEL), F32), jax.ShapeDtypeStruct((t, LANE), F32)],
        compiler_params=_params(("parallel",)),
        name="merge",
    )(h, gates, oa, cb, oc, lp["w_pa"], lp["w_pb"], lp["w_pc"], lp["w_out"], lp["ln1_g"], lp["ln1_b"],
      lp["xa_wq"], mem_k, mem_v, lp["xa_wo"], lp["ln2_g"], lp["ln2_b"], lp["w_route_hi"],
      lp["w_route_lo"], lp["b_route"])


def _moe_kernel(x_ref, comb_ref, wg_ref, wu_ref, wd_ref, lg_ref, lb_ref, o_ref, xb_ref, acc_ref, *, alpha):
    e = pl.program_id(1)

    @pl.when(e == 0)
    def _():
        xb_ref[...] = x_ref[...].astype(BF16)
        acc_ref[...] = jnp.zeros_like(acc_ref)

    xb = xb_ref[...]
    lane = lax.broadcasted_iota(jnp.int32, comb_ref.shape, 1)
    cw = jnp.sum(jnp.where(lane == e + MOE_GROUPS, comb_ref[...], 0.0), axis=-1, keepdims=True)
    a = _dot(xb, wg_ref[0, 0].astype(BF16))
    b = _dot(xb, wu_ref[0, 0].astype(BF16))
    hid = _silu(a) * b * cw
    acc_ref[...] += _dot(hid.astype(BF16), wd_ref[0, 0].astype(BF16))

    @pl.when(e == MOE_EXPERTS - 1)
    def _():
        o_ref[...] = _layer_norm(alpha * x_ref[...] + acc_ref[...], lg_ref[...], lb_ref[...])


def _moe(x, comb, lp, layer, alpha):
    t = x.shape[0]
    tm = min(t, 1024)
    assert t % tm == 0
    row = lambda n: pl.BlockSpec((tm, n), lambda i, e: (i, 0))
    return pl.pallas_call(
        functools.partial(_moe_kernel, alpha=alpha),
        grid=(t // tm, MOE_EXPERTS),
        in_specs=[row(D_MODEL), row(LANE),
                  pl.BlockSpec((1, 1, D_MODEL, MOE_FF), lambda i, e: (layer, e, 0, 0)),
                  pl.BlockSpec((1, 1, D_MODEL, MOE_FF), lambda i, e: (layer, e, 0, 0)),
                  pl.BlockSpec((1, 1, MOE_FF, D_MODEL), lambda i, e: (layer, e, 0, 0)),
                  pl.BlockSpec((1, D_MODEL), lambda i, e: (0, 0)),
                  pl.BlockSpec((1, D_MODEL), lambda i, e: (0, 0))],
        out_specs=row(D_MODEL),
        out_shape=jax.ShapeDtypeStruct((t, D_MODEL), F32),
        scratch_shapes=[pltpu.VMEM((tm, D_MODEL), BF16), pltpu.VMEM((tm, D_MODEL), F32)],
        compiler_params=_params(("parallel", "arbitrary")),
        name="moe",
    )(x, comb, lp["moe_wg"], lp["moe_wu"], lp["moe_wd"], lp["ln3_g"], lp["ln3_b"])


def _memkv_kernel(x_ref, wk_ref, wv_ref, k_ref, v_ref):
    xb = x_ref[...].astype(BF16)
    k_ref[...] = _dot(xb, wk_ref[...])
    v_ref[...] = _dot(xb, wv_ref[...])


def _memkv(mem, lp):
    t = mem.shape[0]
    full = lambda r, c: pl.BlockSpec((r, c), lambda i: (0, 0))
    return pl.pallas_call(
        _memkv_kernel,
        grid=(1,),
        in_specs=[full(t, D_MODEL), full(D_MODEL, D_MODEL), full(D_MODEL, D_MODEL)],
        out_specs=[full(t, D_MODEL), full(t, D_MODEL)],
        out_shape=[jax.ShapeDtypeStruct((t, D_MODEL), F32)] * 2,
        compiler_params=_params(("arbitrary",)),
        name="memkv",
    )(mem, lp["xa_wk"], lp["xa_wv"])


def _rot_half_cols(w):
    half = w.shape[-1] // 2
    return jnp.concatenate([-w[..., half:], w[..., :half]], axis=-1)


def _pack_layer(l, p):
    d = D_MODEL
    cols, start = [], 0
    for s in IN_SIZES:
        cols.append(p["w_in"][l][:, start:start + s])
        start += s
    w_gate, w_cq, w_ckv, w_kr, w_glu, w_qkv, w_gz, w_ga, w_gb = cols
    zpad = lambda n: jnp.zeros((d, n), F32)
    w_in = jnp.concatenate([
        w_gate, w_cq, w_ckv,
        w_kr, zpad(LANE - MLA_ROPE),
        _rot_half_cols(w_kr), zpad(LANE - MLA_ROPE),
        w_glu, w_qkv, w_gz,
        w_ga, w_gb, zpad(LANE - 2 * GDN_HEADS)], axis=1).astype(BF16)
    assert w_in.shape[1] == IN_PACKED

    wuq = p["mla_w_uq"][l].reshape(MLA_Q_LORA, MLA_HEADS, MLA_NOPE + MLA_ROPE)
    nope, rope = wuq[..., :MLA_NOPE], wuq[..., MLA_NOPE:]
    pad_q = HEAD_SLOT - MLA_NOPE - MLA_ROPE
    zq = lambda n: jnp.zeros((MLA_Q_LORA, MLA_HEADS, n), F32)
    wq = jnp.concatenate([nope, rope, zq(pad_q)], -1).reshape(MLA_Q_LORA, MLA_HEADS * HEAD_SLOT)
    wqr = jnp.concatenate([zq(MLA_NOPE), _rot_half_cols(rope), zq(pad_q)], -1).reshape(
        MLA_Q_LORA, MLA_HEADS * HEAD_SLOT)

    wukv = p["mla_w_ukv"][l].reshape(MLA_KV_LORA, MLA_HEADS, MLA_NOPE + MLA_V)
    wk = jnp.concatenate([wukv[..., :MLA_NOPE],
                          jnp.zeros((MLA_KV_LORA, MLA_HEADS, HEAD_SLOT - MLA_NOPE), F32)], -1)
    wk = wk.reshape(MLA_KV_LORA, MLA_HEADS * HEAD_SLOT)
    wv_t = jnp.concatenate([wukv[..., MLA_NOPE:],
                            jnp.zeros((MLA_KV_LORA, MLA_HEADS, VT_ROWS - MLA_V), F32)], -1)
    wv_t = wv_t.reshape(MLA_KV_LORA, MLA_HEADS * VT_ROWS).T
    place = jnp.concatenate([jnp.zeros((MLA_ROPE, ROPE_LANE0), F32), jnp.eye(MLA_ROPE, dtype=F32),
                             jnp.zeros((MLA_ROPE, pad_q), F32)], -1)
    pk = jnp.tile(place, (1, MLA_HEADS))
    w_abs = jnp.concatenate([jnp.transpose(wukv[..., :MLA_NOPE], (1, 2, 0)),
                             jnp.zeros((MLA_HEADS, HEAD_SLOT - MLA_NOPE, MLA_KV_LORA), F32)], axis=1)
    head_cols = jnp.eye(MLA_HEADS, dtype=F32)[:, None, :, None]
    wuv_placed = (jnp.transpose(wukv[..., MLA_NOPE:], (1, 0, 2))[:, :, None, :] * head_cols).reshape(
        MLA_HEADS, MLA_KV_LORA, MLA_HEADS * MLA_V)

    half = MLA_ROPE // 2
    inv = ROPE_THETA ** (-jnp.arange(half, dtype=F32) / half)
    inv2 = jnp.concatenate([inv, inv])
    invf = jnp.stack([
        jnp.concatenate([jnp.zeros((ROPE_LANE0,), F32), inv2, jnp.zeros((pad_q,), F32)]),
        jnp.concatenate([inv2, jnp.zeros((LANE - MLA_ROPE,), F32)])])

    lane_pad = lambda v: jnp.concatenate([v, jnp.zeros((LANE - v.shape[0],), F32)])[None, :]
    w_route = jnp.concatenate([p["moe_w_rg"][l], p["moe_w_re"][l],
                               zpad(LANE - MOE_GROUPS - MOE_EXPERTS)], axis=1)
    w_route_hi = w_route.astype(BF16)
    w_route_lo = (w_route - w_route_hi.astype(F32)).astype(BF16)
    b_route = lane_pad(jnp.concatenate([p["moe_b_rg"][l], p["moe_b_re"][l]]))
    r1 = lambda v: v.reshape(1, -1)
    return dict(
        ln_in_g=r1(p["ln_in_g"]), ln_in_b=r1(p["ln_in_b"]),
        w_in=w_in, b_gate=r1(p["b_gate"][l]), q_norm=r1(p["mla_q_norm"][l]),
        wq=wq.astype(BF16), wqr=wqr.astype(BF16), kv_norm=r1(p["mla_kv_norm"][l]), invf=invf,
        wk=wk.astype(BF16), pk=pk.astype(BF16), wv_t=wv_t.astype(BF16),
        w_abs=w_abs.astype(BF16), pk1=place.astype(BF16), wuv_placed=wuv_placed.astype(BF16),
        cm_w_dw=jnp.broadcast_to(p["cm_w_dw"][l][:, None, :], (CM_WIDTH, SUBLANE, CM_CH)),
        cm_b_dw=r1(p["cm_b_dw"][l]), cm_ln_g=r1(p["cm_ln_g"][l]),
        cm_ln_b=r1(p["cm_ln_b"][l]),
        g_w_conv=p["gdn_w_conv"][l], g_a_log=lane_pad(p["gdn_a_log"][l]),
        g_dt_bias=lane_pad(p["gdn_dt_bias"][l]), g_norm=r1(p["gdn_norm"][l]),
        w_pa=p["w_proj_a"][l].astype(BF16), w_pb=p["w_proj_b"][l].astype(BF16),
        w_pc=p["w_proj_c"][l].astype(BF16), w_out=p["w_out"][l].astype(BF16),
        ln1_g=r1(p["ln1_g"][l]), ln1_b=r1(p["ln1_b"][l]),
        xa_wq=p["xa_w_q"][l].astype(BF16), xa_wk=p["xa_w_k"][l].astype(BF16),
        xa_wv=p["xa_w_v"][l].astype(BF16), xa_wo=p["xa_w_o"][l].astype(BF16),
        ln2_g=r1(p["ln2_g"][l]), ln2_b=r1(p["ln2_b"][l]),
        w_route_hi=w_route_hi, w_route_lo=w_route_lo, b_route=b_route,
        moe_wg=p["moe_w_gate"], moe_wu=p["moe_w_up"], moe_wd=p["moe_w_down"],
        ln3_g=r1(p["ln3_g"][l]), ln3_b=r1(p["ln3_b"][l]),
    )


def _front_pad_rows(state, rows):
    b, r, c = state.shape
    return jnp.concatenate([jnp.zeros((b, rows - r, c), state.dtype), state], axis=1)


def _layer(x, lp, *, layer, batch, seq_len, past, alpha, cache, cm_state, gc_state, s0, mem_k, mem_v):
    first = layer == 0
    res = _inproj(x, lp, pre_ln=first, seq_len=seq_len, past=past)
    if first:
        h, res = res[0], res[1:]
    else:
        h = x
    gates, qp, ckv, kr, glu, qkv, gz, gab = res
    if cache is None:
        kn, vn_t = _kvup(ckv, kr, lp)
        oa = _attn_prompt(qp, kn, vn_t, batch, seq_len)
    else:
        ckv_cache, kr_cache = cache
        oa = _attn_sample(qp, ckv, kr, ckv_cache, kr_cache, layer, lp, batch, seq_len)
    cb = _conformer(glu, _front_pad_rows(cm_state, CM_HALO), lp, batch, seq_len)
    oc, s_new = _gdn(qkv, gab, gz, _front_pad_rows(gc_state, GDN_HALO), s0, lp, batch, seq_len)
    h2, comb = _merge(h, gates, oa, cb, oc, mem_k, mem_v, lp, batch, seq_len, alpha)
    h3 = _moe(h2, comb, lp, layer, alpha)
    assert seq_len >= CM_WIDTH - 1
    states = (ckv.reshape(batch, seq_len, MLA_KV_LORA), kr.reshape(batch, seq_len, MLA_ROPE),
              glu.reshape(batch, seq_len, CM_CH)[:, seq_len - (CM_WIDTH - 1):],
              qkv.reshape(batch, seq_len, GDN_QKV)[:, seq_len - (GDN_CONV - 1):], s_new)
    return h3, states


def kernel(x_prompt, x_sample, mem_prompt, cache_mla_ckv, cache_mla_krope, state_cm_conv, state_gdn_conv, state_gdn, cache_mem_k, cache_mem_v, ln_in_g, ln_in_b, w_in, b_gate, mla_q_norm, mla_w_uq, mla_kv_norm, mla_w_ukv, w_proj_a, cm_w_dw, cm_b_dw, cm_ln_g, cm_ln_b, w_proj_b, gdn_w_conv, gdn_a_log, gdn_dt_bias, gdn_norm, w_proj_c, w_out, ln1_g, ln1_b, xa_w_q, xa_w_k, xa_w_v, xa_w_o, ln2_g, ln2_b, moe_w_rg, moe_b_rg, moe_w_re, moe_b_re, moe_w_gate, moe_w_up, moe_w_down, ln3_g, ln3_b):
    p = dict(ln_in_g=ln_in_g, ln_in_b=ln_in_b, w_in=w_in, b_gate=b_gate, mla_q_norm=mla_q_norm,
             mla_w_uq=mla_w_uq, mla_kv_norm=mla_kv_norm, mla_w_ukv=mla_w_ukv, w_proj_a=w_proj_a,
             cm_w_dw=cm_w_dw, cm_b_dw=cm_b_dw, cm_ln_g=cm_ln_g, cm_ln_b=cm_ln_b, w_proj_b=w_proj_b,
             gdn_w_conv=gdn_w_conv, gdn_a_log=gdn_a_log, gdn_dt_bias=gdn_dt_bias, gdn_norm=gdn_norm,
             w_proj_c=w_proj_c, w_out=w_out, ln1_g=ln1_g, ln1_b=ln1_b, xa_w_q=xa_w_q, xa_w_k=xa_w_k,
             xa_w_v=xa_w_v, xa_w_o=xa_w_o, ln2_g=ln2_g, ln2_b=ln2_b, moe_w_rg=moe_w_rg,
             moe_b_rg=moe_b_rg, moe_w_re=moe_w_re, moe_b_re=moe_b_re, moe_w_gate=moe_w_gate,
             moe_w_up=moe_w_up, moe_w_down=moe_w_down, ln3_g=ln3_g, ln3_b=ln3_b)
    depth = w_in.shape[0]
    alpha = (2 * depth) ** 0.25
    b_p, l_p, d = x_prompt.shape
    b_s, l_s, _ = x_sample.shape
    past = cache_mla_ckv.shape[2]
    mem_len = mem_prompt.shape[1]
    hp = x_prompt.reshape(b_p * l_p, d)
    hs = x_sample.reshape(b_s * l_s, d)
    cm0 = jnp.zeros((b_p, CM_WIDTH - 1, CM_CH), F32)
    gc0 = jnp.zeros((b_p, GDN_CONV - 1, GDN_QKV), F32)
    s0 = jnp.zeros((b_p, GDN_HEADS, GDN_DK, GDN_DV), F32)
    outs_p = [[] for _ in range(7)]
    outs_s = [[] for _ in range(5)]
    for l in range(depth):
        lp = _pack_layer(l, p)
        mk, mv = _memkv(mem_prompt.reshape(b_p * mem_len, d), lp)
        mk = mk.reshape(b_p, mem_len, d)
        mv = mv.reshape(b_p, mem_len, d)
        hp, st_p = _layer(hp, lp, layer=l, batch=b_p, seq_len=l_p, past=0, alpha=alpha,
                          cache=None, cm_state=cm0, gc_state=gc0, s0=s0, mem_k=mk, mem_v=mv)
        hs, st_s = _layer(hs, lp, layer=l, batch=b_s, seq_len=l_s, past=past, alpha=alpha,
                          cache=(cache_mla_ckv, cache_mla_krope), cm_state=state_cm_conv[l],
                          gc_state=state_gdn_conv[l], s0=state_gdn[l],
                          mem_k=cache_mem_k[l].reshape(b_s, mem_len, d),
                          mem_v=cache_mem_v[l].reshape(b_s, mem_len, d))
        mk4 = mk.reshape(b_p, mem_len, XA_HEADS, XA_DIM)
        mv4 = mv.reshape(b_p, mem_len, XA_HEADS, XA_DIM)
        for lst, arr in zip(outs_p, st_p + (mk4, mv4)):
            lst.append(arr)
        for lst, arr in zip(outs_s, st_s):
            lst.append(arr)
    ckv_p, kr_p, cm_p, gc_p, gdn_p, mk_p, mv_p = [jnp.stack(a) for a in outs_p]
    ckv_s, kr_s, cm_s, gc_s, gdn_s = [jnp.stack(a) for a in outs_s]
    return (hp.reshape(b_p, l_p, d), hs.reshape(b_s, l_s, d), ckv_p, kr_p, ckv_s, kr_s,
            cm_p, cm_s, gc_p, gc_s, gdn_p, gdn_s, mk_p, mv_p)
```
